```python
import math
import jax, jax.numpy as jnp
from jax import lax
import numpy as np

D_MODEL = 1024
BATCH = 32
SEQ = 2048
DEPTH = 1
DEC_BATCH = 4
DEC_SEQ = 4096
PAST_LEN = 128

HEAD_DIM = 64
N_Q_A = 8
N_KV_A = 2
GROUP_A = N_Q_A // N_KV_A
N_HEADS_B = 8
WIDTH_A = N_Q_A * HEAD_DIM
WIDTH_B = N_HEADS_B * HEAD_DIM
MIX_WIDTH = WIDTH_A + WIDTH_B
KV_WIDTH_A = N_KV_A * HEAD_DIM
IN_PROJ_WIDTH = WIDTH_A + 2 * KV_WIDTH_A + 3 * WIDTH_B
Q_BLOCK = 128
GRID_W = 64
ROPE_AXIS_DIM = HEAD_DIM // 2
ROPE_THETA = 10000.0
NA_KH = 8
NA_KW = 16
D_FF = 2816
CONV_W = 3
PLE_DIM = 256
EPS = 1e-6

kernel_name = "hymba_gqa_natten_convffn_ple_encoder"


def _rmsnorm(x, g):
    x32 = x.astype(jnp.float32)
    y = x32 * lax.rsqrt(jnp.mean(x32 * x32, axis=-1, keepdims=True) + EPS)
    return (y * g.astype(jnp.float32)).astype(x.dtype)


def _rope_tables(seq):
    t = jnp.arange(seq)
    row = (t // GRID_W).astype(jnp.float32)
    col = (t % GRID_W).astype(jnp.float32)
    freqs = ROPE_THETA ** (-jnp.arange(0, ROPE_AXIS_DIM, 2, dtype=jnp.float32) / ROPE_AXIS_DIM)
    ang_r = row[:, None] * freqs[None]
    ang_c = col[:, None] * freqs[None]
    return jnp.cos(ang_r), jnp.sin(ang_r), jnp.cos(ang_c), jnp.sin(ang_c)


def _rotate(x, cos, sin):
    cos = cos[None, :, None, :].astype(x.dtype)
    sin = sin[None, :, None, :].astype(x.dtype)
    x1, x2 = jnp.split(x, 2, axis=-1)
    return jnp.concatenate([x1 * cos - x2 * sin, x2 * cos + x1 * sin], axis=-1)


def _axial_rope(x, tables):
    cos_r, sin_r, cos_c, sin_c = tables
    xr, xc = x[..., :ROPE_AXIS_DIM], x[..., ROPE_AXIS_DIM:]
    return jnp.concatenate([_rotate(xr, cos_r, sin_r), _rotate(xc, cos_c, sin_c)], axis=-1)


def _gqa_blocked(q, k, v):
    b, s, _, dh = q.shape
    nb = s // Q_BLOCK
    scale = 1.0 / math.sqrt(dh)
    qb = q.reshape(b, nb, Q_BLOCK, N_KV_A, GROUP_A, dh).transpose(1, 0, 2, 3, 4, 5)

    def block(qblk):
        sc = jnp.einsum('bqkgd,bskd->bkgqs', qblk, k, preferred_element_type=jnp.float32) * scale
        p = jax.nn.softmax(sc, axis=-1).astype(v.dtype)
        return jnp.einsum('bkgqs,bskd->bqkgd', p, v)

    o = lax.map(block, qb)
    return o.transpose(1, 0, 2, 3, 4, 5).reshape(b, s, WIDTH_A)


def _neighbourhood_attention(q, k, v, rpb):
    b, s, h, dh = q.shape
    rows = s // GRID_W
    kh = min(NA_KH, rows)
    scale = 1.0 / math.sqrt(dh)
    qg = q.reshape(b, rows, GRID_W, h, dh)
    kg = k.reshape(b, rows, GRID_W, h, dh)
    vg = v.reshape(b, rows, GRID_W, h, dh)
    r_idx = jnp.arange(rows)
    r_start = jnp.clip(r_idx - kh // 2, 0, rows - kh)
    c = jnp.arange(GRID_W)
    c_start = jnp.clip(c - NA_KW // 2, 0, GRID_W - NA_KW)
    kc = c
    col_ok = (kc[None, :] >= c_start[:, None]) & (kc[None, :] < c_start[:, None] + NA_KW)
    dc_idx = jnp.clip(kc[None, :] - c[:, None] + NA_KW - 1, 0, 2 * NA_KW - 2)
    rpb_c = rpb[:, :, dc_idx].astype(jnp.float32)

    def row_fn(args):
        r, r0, q_row = args
        k_strip = lax.dynamic_slice_in_dim(kg, r0, kh, axis=1)
        v_strip = lax.dynamic_slice_in_dim(vg, r0, kh, axis=1)
        dr_idx = r0 + jnp.arange(kh) - r + NA_KH - 1
        bias = rpb_c[:, dr_idx].transpose(0, 2, 1, 3)
        sc = jnp.einsum('bqhd,bikhd->bhqik', q_row, k_strip,
                        preferred_element_type=jnp.float32) * scale + bias[None]
        sc = jnp.where(col_ok[None, None, :, None, :], sc, -jnp.inf)
        p = jax.nn.softmax(sc.reshape(b, h, GRID_W, kh * GRID_W), axis=-1)
        p = p.reshape(b, h, GRID_W, kh, GRID_W).astype(v.dtype)
        return jnp.einsum('bhqik,bikhd->bqhd', p, v_strip)

    o = lax.map(row_fn, (r_idx, r_start, qg.transpose(1, 0, 2, 3, 4)))
    return o.transpose(1, 0, 2, 3, 4).reshape(b, s, WIDTH_B)


def _dwconv_centred(u, w, bias):
    up = jnp.pad(u, ((0, 0), (1, 1), (0, 0)))
    return up[:, :-2] * w[0] + up[:, 1:-1] * w[1] + up[:, 2:] * w[2] + bias


def _trunk(x, p, attn_norm, w_in, q_norm_a, k_norm_a, rpb_b, out_norm_a, out_norm_b,
           w_out, ffn_norm, w_up, conv_w, conv_b, w_down, ple_norm, w_ple_gate, w_ple,
           final_norm):
    b, s, _ = x.shape
    tables = _rope_tables(s)
    h = x
    for i in range(DEPTH):
        n = _rmsnorm(h, attn_norm[i])
        proj = n @ w_in[i]
        splits = np.cumsum([WIDTH_A, KV_WIDTH_A, KV_WIDTH_A, WIDTH_B, WIDTH_B])
        qa, ka, va, qb, kb, vb = jnp.split(proj, splits, axis=-1)
        qa = _rmsnorm(qa.reshape(b, s, N_Q_A, HEAD_DIM), q_norm_a[i])
        ka = _rmsnorm(ka.reshape(b, s, N_KV_A, HEAD_DIM), k_norm_a[i])
        va = va.reshape(b, s, N_KV_A, HEAD_DIM)
        qa = _axial_rope(qa, tables)
        ka = _axial_rope(ka, tables)
        out_a = _gqa_blocked(qa, ka, va)
        out_b = _neighbourhood_attention(qb.reshape(b, s, N_HEADS_B, HEAD_DIM),
                                         kb.reshape(b, s, N_HEADS_B, HEAD_DIM),
                                         vb.reshape(b, s, N_HEADS_B, HEAD_DIM), rpb_b[i])
        mixed = jnp.concatenate([_rmsnorm(out_a, out_norm_a[i]),
                                 _rmsnorm(out_b, out_norm_b[i])], axis=-1)
        h = h + mixed @ w_out[i]
        n2 = _rmsnorm(h, ffn_norm[i])
        u = _dwconv_centred(n2 @ w_up[i], conv_w[i], conv_b[i])
        a, g = jnp.split(u, 2, axis=-1)
        h = h + (a * jax.nn.gelu(g, approximate=True)) @ w_down[i]
        gate = jax.nn.sigmoid(_rmsnorm(h, ple_norm[i]) @ w_ple_gate[i])
        h = h + gate * (p[i] @ w_ple[i])
    return _rmsnorm(h, final_norm)


def setup_inputs(seed: int = 0) -> dict:
    key = jax.random.key(seed)
    ks = jax.random.split(key, 24)
    f32 = jnp.float32

    def nrm(k, shape, scale):
        return jax.random.normal(k, shape, f32) * scale

    def gain(k, shape):
        return 1.0 + 0.05 * jax.random.normal(k, shape, f32)

    return {
        "x_prompt": nrm(ks[0], (BATCH, SEQ, D_MODEL), 1.0),
        "x_sample": nrm(ks[1], (DEC_BATCH, DEC_SEQ, D_MODEL), 1.0),
        "p_prompt": nrm(ks[2], (DEPTH, BATCH, SEQ, PLE_DIM), 1.0),
        "p_sample": nrm(ks[3], (DEPTH, DEC_BATCH, DEC_SEQ, PLE_DIM), 1.0),
        "attn_norm": gain(ks[4], (DEPTH, D_MODEL)),
        "w_in": nrm(ks[5], (DEPTH, D_MODEL, IN_PROJ_WIDTH), D_MODEL ** -0.5),
        "q_norm_a": gain(ks[6], (DEPTH, HEAD_DIM)),
        "k_norm_a": gain(ks[7], (DEPTH, HEAD_DIM)),
        "rpb_b": nrm(ks[8], (DEPTH, N_HEADS_B, 2 * NA_KH - 1, 2 * NA_KW - 1), 0.1),
        "out_norm_a": gain(ks[9], (DEPTH, WIDTH_A)),
        "out_norm_b": gain(ks[10], (DEPTH, WIDTH_B)),
        "w_out": nrm(ks[11], (DEPTH, MIX_WIDTH, D_MODEL), MIX_WIDTH ** -0.5),
        "ffn_norm": gain(ks[12], (DEPTH, D_MODEL)),
        "w_up": nrm(ks[13], (DEPTH, D_MODEL, 2 * D_FF), D_MODEL ** -0.5),
        "conv_w": nrm(ks[14], (DEPTH, CONV_W, 2 * D_FF), CONV_W ** -0.5),
        "conv_b": nrm(ks[15], (DEPTH, 2 * D_FF), 0.02),
        "w_down": nrm(ks[16], (DEPTH, D_FF, D_MODEL), D_FF ** -0.5),
        "ple_norm": gain(ks[17], (DEPTH, D_MODEL)),
        "w_ple_gate": nrm(ks[18], (DEPTH, D_MODEL, D_MODEL), D_MODEL ** -0.5),
        "w_ple": nrm(ks[19], (DEPTH, PLE_DIM, D_MODEL), PLE_DIM ** -0.5),
        "final_norm": gain(ks[20], (D_MODEL,)),
    }


def reference(x_prompt, x_sample, p_prompt, p_sample, attn_norm, w_in, q_norm_a, k_norm_a,
              rpb_b, out_norm_a, out_norm_b, w_out, ffn_norm, w_up, conv_w, conv_b, w_down,
              ple_norm, w_ple_gate, w_ple, final_norm):
    y_prompt = _trunk(x_prompt, p_prompt, attn_norm, w_in, q_norm_a, k_norm_a, rpb_b,
                      out_norm_a, out_norm_b, w_out, ffn_norm, w_up, conv_w, conv_b, w_down,
                      ple_norm, w_ple_gate, w_ple, final_norm)
    y_sample = _trunk(x_sample, p_sample, attn_norm, w_in, q_norm_a, k_norm_a, rpb_b,
                      out_norm_a, out_norm_b, w_out, ffn_norm, w_up, conv_w, conv_b, w_down,
                      ple_norm, w_ple_gate, w_ple, final_norm)
    return (y_prompt, y_sample)
```

```python
import functools
import math

import numpy as np
import jax
import jax.numpy as jnp
from jax import lax
from jax.experimental import pallas as pl
from jax.experimental.pallas import tpu as pltpu

F32 = jnp.float32
BF16 = jnp.bfloat16

D_MODEL = 1024
HEAD_DIM = 64
N_Q_A = 8
N_KV_A = 2
N_HEADS_B = 8
WIDTH_A = N_Q_A * HEAD_DIM
WIDTH_B = N_HEADS_B * HEAD_DIM
KV_WIDTH_A = N_KV_A * HEAD_DIM
IN_PROJ_WIDTH = WIDTH_A + 2 * KV_WIDTH_A + 3 * WIDTH_B
GRID_W = 64
ROPE_AXIS_DIM = HEAD_DIM // 2
ROPE_HALF = ROPE_AXIS_DIM // 2
ROPE_THETA = 10000.0
NA_KH = 8
NA_KW = 16
D_FF = 2816
PLE_DIM = 256
EPS = 1e-6
ATTN_SCALE = 1.0 / math.sqrt(HEAD_DIM)

LANES = 128
BF16_SUBLANES = 16
VMEM_LIMIT_BYTES = 56 * 1024 * 1024

N_TILES = WIDTH_A // LANES
MASK_VALUE = -1e30

TM_PROJ = 512
TQ_GQA = 512
NA_QROWS = 4
NA_WROWS = NA_QROWS + NA_KH
TM_FFN = 512
FF_CHUNK = 256
N_FF_CHUNKS = D_FF // FF_CHUNK
HALO = BF16_SUBLANES


def _rms(x, gain):
    ms = jnp.mean(x * x, axis=-1, keepdims=True)
    return x * lax.rsqrt(ms + EPS) * gain


def _resident(shape):
    zeros = (0,) * len(shape)
    return pl.BlockSpec(shape, lambda *_: zeros, pipeline_mode=pl.Buffered(1))


def _params(n_axes):
    return pltpu.CompilerParams(
        dimension_semantics=("arbitrary",) * n_axes,
        vmem_limit_bytes=VMEM_LIMIT_BYTES,
    )


def _in_proj_kernel(x_ref, g_ref, w_ref, qg_ref, kg_ref, cos_ref, sin_ref, hm_ref,
                    qa_ref, ka_ref, va_ref, qb_ref, kb_ref, vb_ref):
    n = _rms(x_ref[...], g_ref[...]).astype(BF16)
    proj = jnp.dot(n, w_ref[...], preferred_element_type=F32)
    cos = cos_ref[...]
    sin = sin_ref[...]
    lane = lax.broadcasted_iota(jnp.int32, (1, LANES), 1)
    first_half = (lane & ROPE_HALF) == 0
    head_mean = hm_ref[...]

    def head_norm(t, gain):
        w = t.shape[1]
        ms = jnp.dot((t * t).astype(BF16), head_mean[:w, :w], preferred_element_type=F32)
        return t * lax.rsqrt(ms + EPS) * gain

    def rope(t):
        ahead = pltpu.roll(t, LANES - ROPE_HALF, axis=1)
        behind = pltpu.roll(t, ROPE_HALF, axis=1)
        return t * cos + jnp.where(first_half, ahead, behind) * sin

    qa = head_norm(proj[:, :WIDTH_A], qg_ref[...])
    for j in range(N_TILES):
        qa_ref[j] = (rope(qa[:, j * LANES:(j + 1) * LANES]) * ATTN_SCALE).astype(BF16)
    off = WIDTH_A
    ka_ref[...] = rope(head_norm(proj[:, off:off + KV_WIDTH_A], kg_ref[...])).astype(BF16)
    off += KV_WIDTH_A
    va_ref[...] = proj[:, off:off + KV_WIDTH_A].astype(BF16)
    off += KV_WIDTH_A
    for j in range(N_TILES):
        lo = off + j * LANES
        qb_ref[j] = (proj[:, lo:lo + LANES] * ATTN_SCALE).astype(BF16)
        kb_ref[j] = proj[:, lo + WIDTH_B:lo + WIDTH_B + LANES].astype(BF16)
        vb_ref[j] = proj[:, lo + 2 * WIDTH_B:lo + 2 * WIDTH_B + LANES].astype(BF16)


def _in_proj(x2, attn_norm, w_in, q_gain, k_gain, cos_t, sin_t, head_mean, seq):
    t = x2.shape[0]
    tm = TM_PROJ
    assert t % tm == 0 and seq % tm == 0
    tiles_per_seq = seq // tm
    row = lambda i: (i, 0)
    tile3 = lambda i: (0, i, 0)
    pos = lambda i: (i % tiles_per_seq, 0)
    tiled = jax.ShapeDtypeStruct((N_TILES, t, LANES), BF16)
    flat = jax.ShapeDtypeStruct((t, KV_WIDTH_A), BF16)
    return pl.pallas_call(
        _in_proj_kernel,
        grid=(t // tm,),
        in_specs=[
            pl.BlockSpec((tm, D_MODEL), row),
            _resident((1, D_MODEL)),
            _resident((D_MODEL, IN_PROJ_WIDTH)),
            _resident((1, WIDTH_A)),
            _resident((1, KV_WIDTH_A)),
            pl.BlockSpec((tm, LANES), pos),
            pl.BlockSpec((tm, LANES), pos),
            _resident((WIDTH_A, WIDTH_A)),
        ],
        out_specs=[
            pl.BlockSpec((N_TILES, tm, LANES), tile3),
            pl.BlockSpec((tm, KV_WIDTH_A), row),
            pl.BlockSpec((tm, KV_WIDTH_A), row),
            pl.BlockSpec((N_TILES, tm, LANES), tile3),
            pl.BlockSpec((N_TILES, tm, LANES), tile3),
            pl.BlockSpec((N_TILES, tm, LANES), tile3),
        ],
        out_shape=[tiled, flat, flat, tiled, tiled, tiled],
        compiler_params=_params(1),
        name="in_proj",
    )(x2, attn_norm, w_in, q_gain, k_gain, cos_t, sin_t, head_mean)


def _softmax_pv(s, v):
    m = jnp.max(s, axis=-1, keepdims=True)
    p = jnp.exp(s - m)
    denom = jnp.sum(p, axis=-1, keepdims=True)
    o = jnp.dot(p.astype(BF16), v, preferred_element_type=F32)
    return o / denom


def _group_norm_store(acc_ref, gain_ref, o_ref):
    tiles = [acc_ref[j] for j in range(N_TILES)]
    ss = sum(jnp.sum(o * o, axis=-1, keepdims=True) for o in tiles)
    inv = lax.rsqrt(ss * (1.0 / (N_TILES * LANES)) + EPS)
    for j in range(N_TILES):
        sl = slice(j * LANES, (j + 1) * LANES)
        o_ref[:, sl] = (tiles[j] * inv * gain_ref[:, sl]).astype(BF16)


_NT = (((1,), (1,)), ((), ()))


def _gqa_kernel(q_ref, k_ref, v_ref, gain_ref, o_ref, acc_ref):
    low = lax.broadcasted_iota(jnp.int32, (1, LANES), 1) < HEAD_DIM

    def tile_body(j, carry):
        qt = q_ref[j]
        zero = jnp.zeros_like(qt)

        def head(mask):
            s = lax.dot_general(jnp.where(mask, qt, zero), k_ref[...], _NT,
                                preferred_element_type=F32)
            return _softmax_pv(s, v_ref[...])

        acc_ref[j] = jnp.where(low, head(low), head(jnp.logical_not(low)))
        return carry

    lax.fori_loop(0, N_TILES, tile_body, 0)
    _group_norm_store(acc_ref, gain_ref, o_ref)


def _gqa(qa, ka, va, gain, batch, seq):
    t = batch * seq
    tq = TQ_GQA
    assert seq % tq == 0
    nq = seq // tq
    return pl.pallas_call(
        _gqa_kernel,
        grid=(batch, nq),
        in_specs=[
            pl.BlockSpec((N_TILES, tq, LANES), lambda b, i: (0, b * nq + i, 0)),
            pl.BlockSpec((seq, KV_WIDTH_A), lambda b, i: (b, 0)),
            pl.BlockSpec((seq, KV_WIDTH_A), lambda b, i: (b, 0)),
            _resident((1, WIDTH_A)),
        ],
        out_specs=pl.BlockSpec((tq, WIDTH_A), lambda b, i: (b * nq + i, 0)),
        out_shape=jax.ShapeDtypeStruct((t, WIDTH_A), BF16),
        scratch_shapes=[pltpu.VMEM((N_TILES, tq, LANES), F32)],
        compiler_params=_params(2),
        name="gqa",
    )(qa, ka, va, gain)


def _na_kernel(q_ref, k_ref, v_ref, bias_ref, gain_ref, o_ref, acc_ref, *, rows):
    g = pl.program_id(1)
    n_groups = rows // NA_QROWS
    w0 = jnp.clip(NA_QROWS * g - NA_KH // 2, 0, rows - NA_WROWS)
    start = pl.multiple_of(w0 * GRID_W, GRID_W)
    kind = jnp.where(g == 0, 0, jnp.where(g == n_groups - 1, 2, 1))
    low = lax.broadcasted_iota(jnp.int32, (1, LANES), 1) < HEAD_DIM

    def tile_body(j, carry):
        qt = q_ref[j]
        zero = jnp.zeros_like(qt)
        kw = k_ref[j, pl.ds(start, NA_WROWS * GRID_W), :]
        vw = v_ref[j, pl.ds(start, NA_WROWS * GRID_W), :]

        def head(half, mask):
            s = lax.dot_general(jnp.where(mask, qt, zero), kw, _NT, preferred_element_type=F32)
            return _softmax_pv(s + bias_ref[kind, 2 * j + half], vw)

        acc_ref[j] = jnp.where(low, head(0, low), head(1, jnp.logical_not(low)))
        return carry

    lax.fori_loop(0, N_TILES, tile_body, 0)
    _group_norm_store(acc_ref, gain_ref, o_ref)


def _na_bias_table(rpb):
    c = np.arange(GRID_W)
    c_start = np.clip(c - NA_KW // 2, 0, GRID_W - NA_KW)
    col_ok = (c[None, :] >= c_start[:, None]) & (c[None, :] < c_start[:, None] + NA_KW)
    dc_idx = np.clip(c[None, :] - c[:, None] + NA_KW - 1, 0, 2 * NA_KW - 2)
    rl = np.arange(NA_QROWS)[:, None]
    kl = np.arange(NA_WROWS)[None, :]
    half = NA_KH // 2
    assert NA_QROWS <= half
    first = (kl < NA_KH, kl - rl + NA_KH - 1)
    interior = ((kl - rl >= 0) & (kl - rl < NA_KH), kl - rl + NA_KH - 1 - half)
    last = (kl >= NA_WROWS - NA_KH, kl - rl - NA_WROWS + NA_QROWS + NA_KH - 1)
    row_ok = np.stack([np.broadcast_to(k[0], (NA_QROWS, NA_WROWS)) for k in (first, interior, last)])
    dr_idx = np.stack([np.clip(k[1], 0, 2 * NA_KH - 2) for k in (first, interior, last)])
    rpb_c = rpb.astype(F32)[:, :, dc_idx]
    tbl = rpb_c[:, dr_idx]
    ok = row_ok[None, :, :, :, None, None] & col_ok[None, None, None, None, :, :]
    tbl = jnp.where(ok, tbl, MASK_VALUE)
    tbl = tbl.transpose(1, 0, 2, 4, 3, 5)
    return tbl.reshape(3, N_HEADS_B, NA_QROWS * GRID_W, NA_WROWS * GRID_W)


def _natten(qb, kb, vb, bias, gain, batch, seq):
    t = batch * seq
    rows = seq // GRID_W
    assert seq % GRID_W == 0 and rows % NA_QROWS == 0 and rows >= NA_WROWS
    n_groups = rows // NA_QROWS
    tq = NA_QROWS * GRID_W
    return pl.pallas_call(
        functools.partial(_na_kernel, rows=rows),
        grid=(batch, n_groups),
        in_specs=[
            pl.BlockSpec((N_TILES, tq, LANES), lambda b, g: (0, b * n_groups + g, 0)),
            pl.BlockSpec((N_TILES, seq, LANES), lambda b, g: (0, b, 0)),
            pl.BlockSpec((N_TILES, seq, LANES), lambda b, g: (0, b, 0)),
            _resident(bias.shape),
            _resident((1, WIDTH_B)),
        ],
        out_specs=pl.BlockSpec((tq, WIDTH_B), lambda b, g: (b * n_groups + g, 0)),
        out_shape=jax.ShapeDtypeStruct((t, WIDTH_B), BF16),
        scratch_shapes=[pltpu.VMEM((N_TILES, tq, LANES), F32)],
        compiler_params=_params(2),
        name="natten",
    )(qb, kb, vb, bias, gain)


def _out_proj_kernel(ma_ref, mb_ref, x_ref, w_ref, g_ref, h_ref, n_ref):
    mixed = jnp.concatenate([ma_ref[...], mb_ref[...]], axis=1)
    h = x_ref[...] + jnp.dot(mixed, w_ref[...], preferred_element_type=F32)
    h_ref[...] = h
    n_ref[...] = _rms(h, g_ref[...]).astype(BF16)


def _out_proj(mixed_a, mixed_b, x2, w_out, ffn_norm):
    t = x2.shape[0]
    tm = TM_PROJ
    row = lambda i: (i, 0)
    return pl.pallas_call(
        _out_proj_kernel,
        grid=(t // tm,),
        in_specs=[
            pl.BlockSpec((tm, WIDTH_A), row),
            pl.BlockSpec((tm, WIDTH_B), row),
            pl.BlockSpec((tm, D_MODEL), row),
            _resident((WIDTH_A + WIDTH_B, D_MODEL)),
            _resident((1, D_MODEL)),
        ],
        out_specs=[pl.BlockSpec((tm, D_MODEL), row), pl.BlockSpec((tm, D_MODEL), row)],
        out_shape=[jax.ShapeDtypeStruct((t, D_MODEL), F32), jax.ShapeDtypeStruct((t, D_MODEL), BF16)],
        compiler_params=_params(1),
        name="out_proj",
    )(mixed_a, mixed_b, x2, w_out, ffn_norm)


def _gelu_tanh(x):
    c = math.sqrt(2.0 / math.pi)
    return 0.5 * x * (1.0 + jnp.tanh(c * (x + 0.044715 * (x * x * x))))


def _ffn_kernel(n_ref, prev_ref, next_ref, h_ref, p_ref, wup_ref, cw_ref, cb_ref, wdn_ref,
                pn_ref, wg_ref, wp_ref, fn_ref, y_ref, *, tiles_per_seq):
    i = pl.program_id(0)
    tm = n_ref.shape[0]
    pos = i % tiles_per_seq
    prev = prev_ref[...]
    nxt = next_ref[...]
    prev = jnp.where(pos != 0, prev, jnp.zeros_like(prev))
    nxt = jnp.where(pos != tiles_per_seq - 1, nxt, jnp.zeros_like(nxt))
    n_ext = jnp.concatenate([prev, n_ref[...], nxt], axis=0)

    def chunk(c, acc):
        u = jnp.dot(n_ext, wup_ref[c], preferred_element_type=F32)
        cw = cw_ref[c]
        conv = (u[HALO - 1:HALO - 1 + tm] * cw[0:1]
                + u[HALO:HALO + tm] * cw[1:2]
                + u[HALO + 1:HALO + 1 + tm] * cw[2:3]
                + cb_ref[c])
        act = conv[:, :FF_CHUNK] * _gelu_tanh(conv[:, FF_CHUNK:])
        return acc + jnp.dot(act.astype(BF16), wdn_ref[c], preferred_element_type=F32)

    ffn = lax.fori_loop(0, N_FF_CHUNKS, chunk, jnp.zeros((tm, D_MODEL), F32))
    h = h_ref[...] + ffn
    gate = jax.nn.sigmoid(jnp.dot(_rms(h, pn_ref[...]).astype(BF16), wg_ref[...],
                                  preferred_element_type=F32))
    emb = jnp.dot(p_ref[...].astype(BF16), wp_ref[...], preferred_element_type=F32)
    h = h + gate * emb
    y_ref[...] = _rms(h, fn_ref[...])


def _ffn(n2, h1, p2, w_up_c, conv_w_c, conv_b_c, w_down_c, ple_norm, w_gate, w_ple, final_norm, seq):
    t = n2.shape[0]
    tm = TM_FFN
    assert t % tm == 0 and seq % tm == 0 and tm % HALO == 0
    tiles_per_seq = seq // tm
    halo_per_tile = tm // HALO
    n_halo_blocks = t // HALO
    row = lambda i: (i, 0)
    return pl.pallas_call(
        functools.partial(_ffn_kernel, tiles_per_seq=tiles_per_seq),
        grid=(t // tm,),
        in_specs=[
            pl.BlockSpec((tm, D_MODEL), row),
            pl.BlockSpec((HALO, D_MODEL), lambda i: (jnp.maximum(i * halo_per_tile - 1, 0), 0)),
            pl.BlockSpec((HALO, D_MODEL),
                         lambda i: (jnp.minimum((i + 1) * halo_per_tile, n_halo_blocks - 1), 0)),
            pl.BlockSpec((tm, D_MODEL), row),
            pl.BlockSpec((tm, PLE_DIM), row),
            _resident(w_up_c.shape),
            _resident(conv_w_c.shape),
            _resident(conv_b_c.shape),
            _resident(w_down_c.shape),
            _resident((1, D_MODEL)),
            _resident((D_MODEL, D_MODEL)),
            _resident((PLE_DIM, D_MODEL)),
            _resident((1, D_MODEL)),
        ],
        out_specs=pl.BlockSpec((tm, D_MODEL), row),
        out_shape=jax.ShapeDtypeStruct((t, D_MODEL), F32),
        compiler_params=_params(1),
        name="ffn",
    )(n2, n2, n2, h1, p2, w_up_c, conv_w_c, conv_b_c, w_down_c, ple_norm, w_gate, w_ple, final_norm)


def _qa_tile_perm():
    group = N_Q_A // N_KV_A
    cols = []
    for j in range(group):
        for kv in range(N_KV_A):
            head = kv * group + j
            cols.append(np.arange(HEAD_DIM) + head * HEAD_DIM)
    return np.concatenate(cols)


def _rope_tables(seq):
    t = jnp.arange(seq)
    row = (t // GRID_W).astype(F32)
    col = (t % GRID_W).astype(F32)
    freqs = ROPE_THETA ** (-jnp.arange(0, ROPE_AXIS_DIM, 2, dtype=F32) / ROPE_AXIS_DIM)
    ang_r = row[:, None] * freqs[None]
    ang_c = col[:, None] * freqs[None]
    cos_h = jnp.concatenate([jnp.cos(ang_r)] * 2 + [jnp.cos(ang_c)] * 2, axis=1)
    sin_h = jnp.concatenate([-jnp.sin(ang_r), jnp.sin(ang_r), -jnp.sin(ang_c), jnp.sin(ang_c)], axis=1)
    return jnp.concatenate([cos_h, cos_h], axis=1), jnp.concatenate([sin_h, sin_h], axis=1)


def _prepare_params(attn_norm, w_in, q_norm_a, k_norm_a, rpb_b, out_norm_a, out_norm_b, w_out,
                    ffn_norm, w_up, conv_w, conv_b, w_down, ple_norm, w_ple_gate, w_ple, final_norm):
    perm = _qa_tile_perm()
    col_perm = np.concatenate([perm, np.arange(WIDTH_A, IN_PROJ_WIDTH)])
    head_id = np.arange(WIDTH_A) // HEAD_DIM
    head_mean = jnp.asarray((head_id[:, None] == head_id[None, :]) / HEAD_DIM, BF16)
    row2 = lambda v: v.reshape(1, -1).astype(F32)
    a_cols = np.arange(D_FF).reshape(N_FF_CHUNKS, FF_CHUNK)
    up_cols = np.concatenate([a_cols, a_cols + D_FF], axis=1)
    return dict(
        attn_norm=row2(attn_norm[0]),
        w_in=w_in[0][:, col_perm].astype(BF16),
        q_gain=row2(jnp.tile(q_norm_a[0], N_Q_A)),
        k_gain=row2(jnp.tile(k_norm_a[0], N_KV_A)),
        head_mean=head_mean,
        na_bias=_na_bias_table(rpb_b[0]),
        gain_a=row2(out_norm_a[0][perm]),
        gain_b=row2(out_norm_b[0]),
        w_out=jnp.concatenate([w_out[0][:WIDTH_A][perm], w_out[0][WIDTH_A:]], axis=0).astype(BF16),
        ffn_norm=row2(ffn_norm[0]),
        w_up=w_up[0][:, up_cols].transpose(1, 0, 2).astype(BF16),
        conv_w=conv_w[0][:, up_cols].transpose(1, 0, 2).astype(F32),
        conv_b=conv_b[0][up_cols].reshape(N_FF_CHUNKS, 1, 2 * FF_CHUNK).astype(F32),
        w_down=w_down[0].reshape(N_FF_CHUNKS, FF_CHUNK, D_MODEL).astype(BF16),
        ple_norm=row2(ple_norm[0]),
        w_gate=w_ple_gate[0].astype(BF16),
        w_ple=w_ple[0].astype(BF16),
        final_norm=row2(final_norm),
    )


def _trunk(x, p, w):
    batch, seq, _ = x.shape
    t = batch * seq
    x2 = x.reshape(t, D_MODEL)
    p2 = p[0].reshape(t, PLE_DIM)
    cos_t, sin_t = _rope_tables(seq)
    qa, ka, va, qb, kb, vb = _in_proj(x2, w["attn_norm"], w["w_in"], w["q_gain"], w["k_gain"],
                                      cos_t, sin_t, w["head_mean"], seq)
    mixed_a = _gqa(qa, ka, va, w["gain_a"], batch, seq)
    mixed_b = _natten(qb, kb, vb, w["na_bias"], w["gain_b"], batch, seq)
    h1, n2 = _out_proj(mixed_a, mixed_b, x2, w["w_out"], w["ffn_norm"])
    y = _ffn(n2, h1, p2, w["w_up"], w["conv_w"], w["conv_b"], w["w_down"], w["ple_norm"],
             w["w_gate"], w["w_ple"], w["final_norm"], seq)
    return y.reshape(batch, seq, D_MODEL)


def kernel(x_prompt, x_sample, p_prompt, p_sample, attn_norm, w_in, q_norm_a, k_norm_a, rpb_b,
           out_norm_a, out_norm_b, w_out, ffn_norm, w_up, conv_w, conv_b, w_down, ple_norm,
           w_ple_gate, w_ple, final_norm):
    w = _prepare_params(attn_norm, w_in, q_norm_a, k_norm_a, rpb_b, out_norm_a, out_norm_b, w_out,
                        ffn_norm, w_up, conv_w, conv_b, w_down, ple_norm, w_ple_gate, w_ple, final_norm)
    return (_trunk(x_prompt, p_prompt, w), _trunk(x_sample, p_sample, w))
```

```python
import functools
import math

import numpy as np
import jax
import jax.numpy as jnp
from jax import lax
from jax.experimental import pallas as pl
from jax.experimental.pallas import tpu as pltpu

F32 = jnp.float32
BF16 = jnp.bfloat16

D_MODEL = 1024
HEAD_DIM = 64
N_Q_A = 8
N_KV_A = 2
N_HEADS_B = 8
WIDTH_A = N_Q_A * HEAD_DIM
WIDTH_B = N_HEADS_B * HEAD_DIM
KV_WIDTH_A = N_KV_A * HEAD_DIM
IN_PROJ_WIDTH = WIDTH_A + 2 * KV_WIDTH_A + 3 * WIDTH_B
GRID_W = 64
ROPE_AXIS_DIM = HEAD_DIM // 2
ROPE_HALF = ROPE_AXIS_DIM // 2
ROPE_THETA = 10000.0
NA_KH = 8
NA_KW = 16
D_FF = 2816
PLE_DIM = 256
EPS = 1e-6
ATTN_SCALE = 1.0 / math.sqrt(HEAD_DIM)

LANES = 128
BF16_SUBLANES = 16
VMEM_LIMIT_BYTES = 56 * 1024 * 1024

N_TILES = WIDTH_A // LANES
MASK_VALUE = -1e30

TM_PROJ = 512
TQ_GQA = 512
NA_QROWS = 4
NA_WROWS = NA_QROWS + NA_KH
TM_FFN = 512
FF_CHUNK = 256
N_FF_CHUNKS = D_FF // FF_CHUNK
HALO = BF16_SUBLANES


def _rms(x, gain):
    ms = jnp.mean(x * x, axis=-1, keepdims=True)
    return x * lax.rsqrt(ms + EPS) * gain


def _resident(shape):
    zeros = (0,) * len(shape)
    return pl.BlockSpec(shape, lambda *_: zeros, pipeline_mode=pl.Buffered(1))


def _params(n_axes):
    return pltpu.CompilerParams(
        dimension_semantics=("arbitrary",) * n_axes,
        vmem_limit_bytes=VMEM_LIMIT_BYTES,
    )


def _in_proj_kernel(x_ref, g_ref, w_ref, qg_ref, kg_ref, cos_ref, sin_ref, hm_ref,
                    qa_ref, ka_ref, va_ref, qb_ref, kb_ref, vb_ref):
    n = _rms(x_ref[...], g_ref[...]).astype(BF16)
    proj = jnp.dot(n, w_ref[...], preferred_element_type=F32)
    cos = cos_ref[...]
    sin = sin_ref[...]
    lane = lax.broadcasted_iota(jnp.int32, (1, LANES), 1)
    first_half = (lane & ROPE_HALF) == 0
    head_mean = hm_ref[...]

    def head_norm(t, gain):
        w = t.shape[1]
        ms = jnp.dot((t * t).astype(BF16), head_mean[:w, :w], preferred_element_type=F32)
        return t * lax.rsqrt(ms + EPS) * gain

    def rope(t):
        ahead = pltpu.roll(t, LANES - ROPE_HALF, axis=1)
        behind = pltpu.roll(t, ROPE_HALF, axis=1)
        return t * cos + jnp.where(first_half, ahead, behind) * sin

    qa = head_norm(proj[:, :WIDTH_A], qg_ref[...])
    for j in range(N_TILES):
        qa_ref[j] = (rope(qa[:, j * LANES:(j + 1) * LANES]) * ATTN_SCALE).astype(BF16)
    off = WIDTH_A
    ka_ref[...] = rope(head_norm(proj[:, off:off + KV_WIDTH_A], kg_ref[...])).astype(BF16)
    off += KV_WIDTH_A
    va_ref[...] = proj[:, off:off + KV_WIDTH_A].astype(BF16)
    off += KV_WIDTH_A
    for j in range(N_TILES):
        lo = off + j * LANES
        qb_ref[j] = (proj[:, lo:lo + LANES] * ATTN_SCALE).astype(BF16)
        kb_ref[j] = proj[:, lo + WIDTH_B:lo + WIDTH_B + LANES].astype(BF16)
        vb_ref[j] = proj[:, lo + 2 * WIDTH_B:lo + 2 * WIDTH_B + LANES].astype(BF16)


def _in_proj(x2, attn_norm, w_in, q_gain, k_gain, cos_t, sin_t, head_mean, seq):
    t = x2.shape[0]
    tm = TM_PROJ
    assert t % tm == 0 and seq % tm == 0
    tiles_per_seq = seq // tm
    row = lambda i: (i, 0)
    tile3 = lambda i: (0, i, 0)
    pos = lambda i: (i % tiles_per_seq, 0)
    tiled = jax.ShapeDtypeStruct((N_TILES, t, LANES), BF16)
    flat = jax.ShapeDtypeStruct((t, KV_WIDTH_A), BF16)
    return pl.pallas_call(
        _in_proj_kernel,
        grid=(t // tm,),
        in_specs=[
            pl.BlockSpec((tm, D_MODEL), row),
            _resident((1, D_MODEL)),
            _resident((D_MODEL, IN_PROJ_WIDTH)),
            _resident((1, WIDTH_A)),
            _resident((1, KV_WIDTH_A)),
            pl.BlockSpec((tm, LANES), pos),
            pl.BlockSpec((tm, LANES), pos),
            _resident((WIDTH_A, WIDTH_A)),
        ],
        out_specs=[
            pl.BlockSpec((N_TILES, tm, LANES), tile3),
            pl.BlockSpec((tm, KV_WIDTH_A), row),
            pl.BlockSpec((tm, KV_WIDTH_A), row),
            pl.BlockSpec((N_TILES, tm, LANES), tile3),
            pl.BlockSpec((N_TILES, tm, LANES), tile3),
            pl.BlockSpec((N_TILES, tm, LANES), tile3),
        ],
        out_shape=[tiled, flat, flat, tiled, tiled, tiled],
        compiler_params=_params(1),
        name="in_proj",
    )(x2, attn_norm, w_in, q_gain, k_gain, cos_t, sin_t, head_mean)


def _attention_pipeline(heads, score_fn, value_fn, s_ref, p_ref, acc_ref, low):
    n = len(heads)
    denoms = {}
    written = set()
    for t in range(n + 2):
        if t < n:
            s_ref[t % 2] = score_fn(*heads[t])
        if 1 <= t <= n:
            h = t - 1
            s = s_ref[h % 2]
            p = jnp.exp(s - jnp.max(s, axis=-1, keepdims=True))
            denoms[h] = jnp.sum(p, axis=-1, keepdims=True)
            p_ref[h % 2] = p.astype(BF16)
        if t >= 2:
            h = t - 2
            tile, half = heads[h]
            o = jnp.dot(p_ref[h % 2], value_fn(tile), preferred_element_type=F32) / denoms.pop(h)
            if tile in written:
                prev = acc_ref[tile]
                o = jnp.where(low, prev, o) if half else jnp.where(low, o, prev)
            acc_ref[tile] = o
            written.add(tile)


_HEAD_ORDER = [(tile, half) for tile in range(N_TILES) for half in range(2)]


def _group_norm_store(acc_ref, gain_ref, o_ref):
    tiles = [acc_ref[j] for j in range(N_TILES)]
    ss = sum(jnp.sum(o * o, axis=-1, keepdims=True) for o in tiles)
    inv = lax.rsqrt(ss * (1.0 / (N_TILES * LANES)) + EPS)
    for j in range(N_TILES):
        sl = slice(j * LANES, (j + 1) * LANES)
        o_ref[:, sl] = (tiles[j] * inv * gain_ref[:, sl]).astype(BF16)


_NT = (((1,), (1,)), ((), ()))


def _gqa_kernel(q_ref, k_ref, v_ref, gain_ref, o_ref, acc_ref, s_ref, p_ref):
    low = lax.broadcasted_iota(jnp.int32, (1, LANES), 1) < HEAD_DIM

    def scores(tile, half):
        qt = q_ref[tile]
        qm = jnp.where(jnp.logical_not(low) if half else low, qt, jnp.zeros_like(qt))
        return lax.dot_general(qm, k_ref[...], _NT, preferred_element_type=F32)

    _attention_pipeline(_HEAD_ORDER, scores, lambda tile: v_ref[...], s_ref, p_ref, acc_ref, low)
    _group_norm_store(acc_ref, gain_ref, o_ref)


def _gqa(qa, ka, va, gain, batch, seq):
    t = batch * seq
    tq = TQ_GQA
    assert seq % tq == 0
    nq = seq // tq
    return pl.pallas_call(
        _gqa_kernel,
        grid=(batch, nq),
        in_specs=[
            pl.BlockSpec((N_TILES, tq, LANES), lambda b, i: (0, b * nq + i, 0)),
            pl.BlockSpec((seq, KV_WIDTH_A), lambda b, i: (b, 0)),
            pl.BlockSpec((seq, KV_WIDTH_A), lambda b, i: (b, 0)),
            _resident((1, WIDTH_A)),
        ],
        out_specs=pl.BlockSpec((tq, WIDTH_A), lambda b, i: (b * nq + i, 0)),
        out_shape=jax.ShapeDtypeStruct((t, WIDTH_A), BF16),
        scratch_shapes=[pltpu.VMEM((N_TILES, tq, LANES), F32),
                        pltpu.VMEM((2, tq, seq), F32),
                        pltpu.VMEM((2, tq, seq), BF16)],
        compiler_params=_params(2),
        name="gqa",
    )(qa, ka, va, gain)


def _na_kernel(q_ref, k_ref, v_ref, bias_ref, gain_ref, o_ref, acc_ref, s_ref, p_ref, *, rows):
    g = pl.program_id(1)
    n_groups = rows // NA_QROWS
    w0 = jnp.clip(NA_QROWS * g - NA_KH // 2, 0, rows - NA_WROWS)
    start = pl.multiple_of(w0 * GRID_W, GRID_W)
    kind = jnp.where(g == 0, 0, jnp.where(g == n_groups - 1, 2, 1))
    low = lax.broadcasted_iota(jnp.int32, (1, LANES), 1) < HEAD_DIM
    window = pl.ds(start, NA_WROWS * GRID_W)

    def scores(tile, half):
        qt = q_ref[tile]
        qm = jnp.where(jnp.logical_not(low) if half else low, qt, jnp.zeros_like(qt))
        s = lax.dot_general(qm, k_ref[tile, window, :], _NT, preferred_element_type=F32)
        return s + bias_ref[kind, 2 * tile + half]

    _attention_pipeline(_HEAD_ORDER, scores, lambda tile: v_ref[tile, window, :],
                        s_ref, p_ref, acc_ref, low)
    _group_norm_store(acc_ref, gain_ref, o_ref)


def _na_bias_table(rpb):
    c = np.arange(GRID_W)
    c_start = np.clip(c - NA_KW // 2, 0, GRID_W - NA_KW)
    col_ok = (c[None, :] >= c_start[:, None]) & (c[None, :] < c_start[:, None] + NA_KW)
    dc_idx = np.clip(c[None, :] - c[:, None] + NA_KW - 1, 0, 2 * NA_KW - 2)
    rl = np.arange(NA_QROWS)[:, None]
    kl = np.arange(NA_WROWS)[None, :]
    half = NA_KH // 2
    assert NA_QROWS <= half
    first = (kl < NA_KH, kl - rl + NA_KH - 1)
    interior = ((kl - rl >= 0) & (kl - rl < NA_KH), kl - rl + NA_KH - 1 - half)
    last = (kl >= NA_WROWS - NA_KH, kl - rl - NA_WROWS + NA_QROWS + NA_KH - 1)
    row_ok = np.stack([np.broadcast_to(k[0], (NA_QROWS, NA_WROWS)) for k in (first, interior, last)])
    dr_idx = np.stack([np.clip(k[1], 0, 2 * NA_KH - 2) for k in (first, interior, last)])
    rpb_c = rpb.astype(F32)[:, :, dc_idx]
    tbl = rpb_c[:, dr_idx]
    ok = row_ok[None, :, :, :, None, None] & col_ok[None, None, None, None, :, :]
    tbl = jnp.where(ok, tbl, MASK_VALUE)
    tbl = tbl.transpose(1, 0, 2, 4, 3, 5)
    return tbl.reshape(3, N_HEADS_B, NA_QROWS * GRID_W, NA_WROWS * GRID_W)


def _natten(qb, kb, vb, bias, gain, batch, seq):
    t = batch * seq
    rows = seq // GRID_W
    assert seq % GRID_W == 0 and rows % NA_QROWS == 0 and rows >= NA_WROWS
    n_groups = rows // NA_QROWS
    tq = NA_QROWS * GRID_W
    return pl.pallas_call(
        functools.partial(_na_kernel, rows=rows),
        grid=(batch, n_groups),
        in_specs=[
            pl.BlockSpec((N_TILES, tq, LANES), lambda b, g: (0, b * n_groups + g, 0)),
            pl.BlockSpec((N_TILES, seq, LANES), lambda b, g: (0, b, 0)),
            pl.BlockSpec((N_TILES, seq, LANES), lambda b, g: (0, b, 0)),
            _resident(bias.shape),
            _resident((1, WIDTH_B)),
        ],
        out_specs=pl.BlockSpec((tq, WIDTH_B), lambda b, g: (b * n_groups + g, 0)),
        out_shape=jax.ShapeDtypeStruct((t, WIDTH_B), BF16),
        scratch_shapes=[pltpu.VMEM((N_TILES, tq, LANES), F32),
                        pltpu.VMEM((2, tq, NA_WROWS * GRID_W), F32),
                        pltpu.VMEM((2, tq, NA_WROWS * GRID_W), BF16)],
        compiler_params=_params(2),
        name="natten",
    )(qb, kb, vb, bias, gain)


def _out_proj_kernel(ma_ref, mb_ref, x_ref, w_ref, g_ref, h_ref, n_ref):
    mixed = jnp.concatenate([ma_ref[...], mb_ref[...]], axis=1)
    h = x_ref[...] + jnp.dot(mixed, w_ref[...], preferred_element_type=F32)
    h_ref[...] = h
    n_ref[...] = _rms(h, g_ref[...]).astype(BF16)


def _out_proj(mixed_a, mixed_b, x2, w_out, ffn_norm):
    t = x2.shape[0]
    tm = TM_PROJ
    row = lambda i: (i, 0)
    return pl.pallas_call(
        _out_proj_kernel,
        grid=(t // tm,),
        in_specs=[
            pl.BlockSpec((tm, WIDTH_A), row),
            pl.BlockSpec((tm, WIDTH_B), row),
            pl.BlockSpec((tm, D_MODEL), row),
            _resident((WIDTH_A + WIDTH_B, D_MODEL)),
            _resident((1, D_MODEL)),
        ],
        out_specs=[pl.BlockSpec((tm, D_MODEL), row), pl.BlockSpec((tm, D_MODEL), row)],
        out_shape=[jax.ShapeDtypeStruct((t, D_MODEL), F32), jax.ShapeDtypeStruct((t, D_MODEL), BF16)],
        compiler_params=_params(1),
        name="out_proj",
    )(mixed_a, mixed_b, x2, w_out, ffn_norm)


def _gelu_tanh(x):
    c = math.sqrt(2.0 / math.pi)
    return 0.5 * x * (1.0 + jnp.tanh(c * (x + 0.044715 * (x * x * x))))


def _ffn_kernel(n_ref, prev_ref, next_ref, h_ref, p_ref, wup_ref, cw_ref, cb_ref, wdn_ref,
                pn_ref, wg_ref, wp_ref, fn_ref, y_ref, ext_ref, u_ref, act_ref, acc_ref,
                *, tiles_per_seq):
    i = pl.program_id(0)
    tm = n_ref.shape[0]
    pos = i % tiles_per_seq
    prev = prev_ref[...]
    nxt = next_ref[...]
    ext_ref[0:HALO] = jnp.where(pos != 0, prev, jnp.zeros_like(prev))
    ext_ref[HALO:HALO + tm] = n_ref[...]
    ext_ref[HALO + tm:] = jnp.where(pos != tiles_per_seq - 1, nxt, jnp.zeros_like(nxt))

    for t in range(N_FF_CHUNKS + 2):
        if t < N_FF_CHUNKS:
            u_ref[t % 2] = jnp.dot(ext_ref[...], wup_ref[t], preferred_element_type=F32)
        if 1 <= t <= N_FF_CHUNKS:
            c = t - 1
            cw = cw_ref[c]
            conv = (u_ref[c % 2, HALO - 1:HALO - 1 + tm] * cw[0:1]
                    + u_ref[c % 2, HALO:HALO + tm] * cw[1:2]
                    + u_ref[c % 2, HALO + 1:HALO + 1 + tm] * cw[2:3]
                    + cb_ref[c])
            act = conv[:, :FF_CHUNK] * _gelu_tanh(conv[:, FF_CHUNK:])
            act_ref[c % 2] = act.astype(BF16)
        if t >= 2:
            c = t - 2
            d = jnp.dot(act_ref[c % 2], wdn_ref[c], preferred_element_type=F32)
            if c == 0:
                acc_ref[...] = d
            else:
                acc_ref[...] += d

    h = h_ref[...] + acc_ref[...]
    gate = jax.nn.sigmoid(jnp.dot(_rms(h, pn_ref[...]).astype(BF16), wg_ref[...],
                                  preferred_element_type=F32))
    emb = jnp.dot(p_ref[...].astype(BF16), wp_ref[...], preferred_element_type=F32)
    h = h + gate * emb
    y_ref[...] = _rms(h, fn_ref[...])


def _ffn(n2, h1, p2, w_up_c, conv_w_c, conv_b_c, w_down_c, ple_norm, w_gate, w_ple, final_norm, seq):
    t = n2.shape[0]
    tm = TM_FFN
    assert t % tm == 0 and seq % tm == 0 and tm % HALO == 0
    tiles_per_seq = seq // tm
    halo_per_tile = tm // HALO
    n_halo_blocks = t // HALO
    row = lambda i: (i, 0)
    return pl.pallas_call(
        functools.partial(_ffn_kernel, tiles_per_seq=tiles_per_seq),
        grid=(t // tm,),
        in_specs=[
            pl.BlockSpec((tm, D_MODEL), row),
            pl.BlockSpec((HALO, D_MODEL), lambda i: (jnp.maximum(i * halo_per_tile - 1, 0), 0)),
            pl.BlockSpec((HALO, D_MODEL),
                         lambda i: (jnp.minimum((i + 1) * halo_per_tile, n_halo_blocks - 1), 0)),
            pl.BlockSpec((tm, D_MODEL), row),
            pl.BlockSpec((tm, PLE_DIM), row),
            _resident(w_up_c.shape),
            _resident(conv_w_c.shape),
            _resident(conv_b_c.shape),
            _resident(w_down_c.shape),
            _resident((1, D_MODEL)),
            _resident((D_MODEL, D_MODEL)),
            _resident((PLE_DIM, D_MODEL)),
            _resident((1, D_MODEL)),
        ],
        out_specs=pl.BlockSpec((tm, D_MODEL), row),
        out_shape=jax.ShapeDtypeStruct((t, D_MODEL), F32),
        scratch_shapes=[pltpu.VMEM((tm + 2 * HALO, D_MODEL), BF16),
                        pltpu.VMEM((2, tm + 2 * HALO, 2 * FF_CHUNK), F32),
                        pltpu.VMEM((2, tm, FF_CHUNK), BF16),
                        pltpu.VMEM((tm, D_MODEL), F32)],
        compiler_params=_params(1),
        name="ffn",
    )(n2, n2, n2, h1, p2, w_up_c, conv_w_c, conv_b_c, w_down_c, ple_norm, w_gate, w_ple, final_norm)


def _qa_tile_perm():
    group = N_Q_A // N_KV_A
    cols = []
    for j in range(group):
        for kv in range(N_KV_A):
            head = kv * group + j
            cols.append(np.arange(HEAD_DIM) + head * HEAD_DIM)
    return np.concatenate(cols)


def _rope_tables(seq):
    t = jnp.arange(seq)
    row = (t // GRID_W).astype(F32)
    col = (t % GRID_W).astype(F32)
    freqs = ROPE_THETA ** (-jnp.arange(0, ROPE_AXIS_DIM, 2, dtype=F32) / ROPE_AXIS_DIM)
    ang_r = row[:, None] * freqs[None]
    ang_c = col[:, None] * freqs[None]
    cos_h = jnp.concatenate([jnp.cos(ang_r)] * 2 + [jnp.cos(ang_c)] * 2, axis=1)
    sin_h = jnp.concatenate([-jnp.sin(ang_r), jnp.sin(ang_r), -jnp.sin(ang_c), jnp.sin(ang_c)], axis=1)
    return jnp.concatenate([cos_h, cos_h], axis=1), jnp.concatenate([sin_h, sin_h], axis=1)


def _prepare_params(attn_norm, w_in, q_norm_a, k_norm_a, rpb_b, out_norm_a, out_norm_b, w_out,
                    ffn_norm, w_up, conv_w, conv_b, w_down, ple_norm, w_ple_gate, w_ple, final_norm):
    perm = _qa_tile_perm()
    col_perm = np.concatenate([perm, np.arange(WIDTH_A, IN_PROJ_WIDTH)])
    head_id = np.arange(WIDTH_A) // HEAD_DIM
    head_mean = jnp.asarray((head_id[:, None] == head_id[None, :]) / HEAD_DIM, BF16)
    row2 = lambda v: v.reshape(1, -1).astype(F32)
    a_cols = np.arange(D_FF).reshape(N_FF_CHUNKS, FF_CHUNK)
    up_cols = np.concatenate([a_cols, a_cols + D_FF], axis=1)
    return dict(
        attn_norm=row2(attn_norm[0]),
        w_in=w_in[0][:, col_perm].astype(BF16),
        q_gain=row2(jnp.tile(q_norm_a[0], N_Q_A)),
        k_gain=row2(jnp.tile(k_norm_a[0], N_KV_A)),
        head_mean=head_mean,
        na_bias=_na_bias_table(rpb_b[0]),
        gain_a=row2(out_norm_a[0][perm]),
        gain_b=row2(out_norm_b[0]),
        w_out=jnp.concatenate([w_out[0][:WIDTH_A][perm], w_out[0][WIDTH_A:]], axis=0).astype(BF16),
        ffn_norm=row2(ffn_norm[0]),
        w_up=w_up[0][:, up_cols].transpose(1, 0, 2).astype(BF16),
        conv_w=conv_w[0][:, up_cols].transpose(1, 0, 2).astype(F32),
        conv_b=conv_b[0][up_cols].reshape(N_FF_CHUNKS, 1, 2 * FF_CHUNK).astype(F32),
        w_down=w_down[0].reshape(N_FF_CHUNKS, FF_CHUNK, D_MODEL).astype(BF16),
        ple_norm=row2(ple_norm[0]),
        w_gate=w_ple_gate[0].astype(BF16),
        w_ple=w_ple[0].astype(BF16),
        final_norm=row2(final_norm),
    )


def _trunk(x, p, w):
    batch, seq, _ = x.shape
    t = batch * seq
    x2 = x.reshape(t, D_MODEL)
    p2 = p[0].reshape(t, PLE_DIM)
    cos_t, sin_t = _rope_tables(seq)
    qa, ka, va, qb, kb, vb = _in_proj(x2, w["attn_norm"], w["w_in"], w["q_gain"], w["k_gain"],
                                      cos_t, sin_t, w["head_mean"], seq)
    mixed_a = _gqa(qa, ka, va, w["gain_a"], batch, seq)
    mixed_b = _natten(qb, kb, vb, w["na_bias"], w["gain_b"], batch, seq)
    h1, n2 = _out_proj(mixed_a, mixed_b, x2, w["w_out"], w["ffn_norm"])
    y = _ffn(n2, h1, p2, w["w_up"], w["conv_w"], w["conv_b"], w["w_down"], w["ple_norm"],
             w["w_gate"], w["w_ple"], w["final_norm"], seq)
    return y.reshape(batch, seq, D_MODEL)


def kernel(x_prompt, x_sample, p_prompt, p_sample, attn_norm, w_in, q_norm_a, k_norm_a, rpb_b,
           out_norm_a, out_norm_b, w_out, ffn_norm, w_up, conv_w, conv_b, w_down, ple_norm,
           w_ple_gate, w_ple, final_norm):
    w = _prepare_params(attn_norm, w_in, q_norm_a, k_norm_a, rpb_b, out_norm_a, out_norm_b, w_out,
                        ffn_norm, w_up, conv_w, conv_b, w_down, ple_norm, w_ple_gate, w_ple, final_norm)
    return (_trunk(x_prompt, p_prompt, w), _trunk(x_sample, p_sample, w))
```

```python
import functools
import math

import numpy as np
import jax
import jax.numpy as jnp
from jax import lax
from jax.experimental import pallas as pl
from jax.experimental.pallas import tpu as pltpu

F32 = jnp.float32
BF16 = jnp.bfloat16

D_MODEL = 1024
HEAD_DIM = 64
N_Q_A = 8
N_KV_A = 2
N_HEADS_B = 8
WIDTH_A = N_Q_A * HEAD_DIM
WIDTH_B = N_HEADS_B * HEAD_DIM
KV_WIDTH_A = N_KV_A * HEAD_DIM
IN_PROJ_WIDTH = WIDTH_A + 2 * KV_WIDTH_A + 3 * WIDTH_B
GRID_W = 64
ROPE_AXIS_DIM = HEAD_DIM // 2
ROPE_HALF = ROPE_AXIS_DIM // 2
ROPE_THETA = 10000.0
NA_KH = 8
NA_KW = 16
D_FF = 2816
PLE_DIM = 256
EPS = 1e-6
LOG2E = math.log2(math.e)
Q_SCALE = LOG2E / math.sqrt(HEAD_DIM)

LANES = 128
BF16_SUBLANES = 16
VMEM_LIMIT_BYTES = 56 * 1024 * 1024

N_TILES = WIDTH_A // LANES
MASK_VALUE = -1e30

TM_PROJ = 512
TQ_GQA = 512
NA_QROWS = 4
NA_WROWS = NA_QROWS + NA_KH
TM_FFN = 512
FF_CHUNK = 256
N_FF_CHUNKS = D_FF // FF_CHUNK
HALO = BF16_SUBLANES


def _rms(x, gain):
    ms = jnp.mean(x * x, axis=-1, keepdims=True)
    return x * lax.rsqrt(ms + EPS) * gain


def _resident(shape):
    zeros = (0,) * len(shape)
    return pl.BlockSpec(shape, lambda *_: zeros, pipeline_mode=pl.Buffered(1))


def _params(n_axes):
    return pltpu.CompilerParams(
        dimension_semantics=("arbitrary",) * n_axes,
        vmem_limit_bytes=VMEM_LIMIT_BYTES,
    )


def _in_proj_kernel(x_ref, g_ref, w_ref, qg_ref, kg_ref, cos_ref, sin_ref, hm_ref,
                    qa_ref, ka_ref, va_ref, qb_ref, kb_ref, vb_ref):
    n = _rms(x_ref[...], g_ref[...]).astype(BF16)
    proj = jnp.dot(n, w_ref[...], preferred_element_type=F32)
    cos = cos_ref[...]
    sin = sin_ref[...]
    lane = lax.broadcasted_iota(jnp.int32, (1, LANES), 1)
    first_half = (lane & ROPE_HALF) == 0
    head_mean = hm_ref[...]

    def head_norm(t, gain):
        w = t.shape[1]
        ms = jnp.dot((t * t).astype(BF16), head_mean[:w, :w], preferred_element_type=F32)
        return t * lax.rsqrt(ms + EPS) * gain

    def rope(t):
        ahead = pltpu.roll(t, LANES - ROPE_HALF, axis=1)
        behind = pltpu.roll(t, ROPE_HALF, axis=1)
        return t * cos + jnp.where(first_half, ahead, behind) * sin

    qa = head_norm(proj[:, :WIDTH_A], qg_ref[...])
    for j in range(N_TILES):
        qa_ref[j] = (rope(qa[:, j * LANES:(j + 1) * LANES]) * Q_SCALE).astype(BF16)
    off = WIDTH_A
    ka_ref[...] = rope(head_norm(proj[:, off:off + KV_WIDTH_A], kg_ref[...])).astype(BF16)
    off += KV_WIDTH_A
    va_ref[...] = proj[:, off:off + KV_WIDTH_A].astype(BF16)
    off += KV_WIDTH_A
    for j in range(N_TILES):
        lo = off + j * LANES
        qb_ref[j] = (proj[:, lo:lo + LANES] * Q_SCALE).astype(BF16)
        kb_ref[j] = proj[:, lo + WIDTH_B:lo + WIDTH_B + LANES].astype(BF16)
        vb_ref[j] = proj[:, lo + 2 * WIDTH_B:lo + 2 * WIDTH_B + LANES].astype(BF16)


def _in_proj(x2, attn_norm, w_in, q_gain, k_gain, cos_t, sin_t, head_mean, seq):
    t = x2.shape[0]
    tm = TM_PROJ
    assert t % tm == 0 and seq % tm == 0
    tiles_per_seq = seq // tm
    row = lambda i: (i, 0)
    tile3 = lambda i: (0, i, 0)
    pos = lambda i: (i % tiles_per_seq, 0)
    tiled = jax.ShapeDtypeStruct((N_TILES, t, LANES), BF16)
    flat = jax.ShapeDtypeStruct((t, KV_WIDTH_A), BF16)
    return pl.pallas_call(
        _in_proj_kernel,
        grid=(t // tm,),
        in_specs=[
            pl.BlockSpec((tm, D_MODEL), row),
            _resident((1, D_MODEL)),
            _resident((D_MODEL, IN_PROJ_WIDTH)),
            _resident((1, WIDTH_A)),
            _resident((1, KV_WIDTH_A)),
            pl.BlockSpec((tm, LANES), pos),
            pl.BlockSpec((tm, LANES), pos),
            _resident((WIDTH_A, WIDTH_A)),
        ],
        out_specs=[
            pl.BlockSpec((N_TILES, tm, LANES), tile3),
            pl.BlockSpec((tm, KV_WIDTH_A), row),
            pl.BlockSpec((tm, KV_WIDTH_A), row),
            pl.BlockSpec((N_TILES, tm, LANES), tile3),
            pl.BlockSpec((N_TILES, tm, LANES), tile3),
            pl.BlockSpec((N_TILES, tm, LANES), tile3),
        ],
        out_shape=[tiled, flat, flat, tiled, tiled, tiled],
        compiler_params=_params(1),
        name="in_proj",
    )(x2, attn_norm, w_in, q_gain, k_gain, cos_t, sin_t, head_mean)


def _attention_pipeline(heads, score_fn, value_fn, s_refs, p_refs, acc_ref, low):
    n = len(heads)
    denoms = {}
    written = set()
    for t in range(n + 2):
        if t >= 2:
            h = t - 2
            tile, half = heads[h]
            o = jnp.dot(p_refs[h % 2][...], value_fn(tile), preferred_element_type=F32) / denoms.pop(h)
            if tile in written:
                prev = acc_ref[tile]
                o = jnp.where(low, prev, o) if half else jnp.where(low, o, prev)
            acc_ref[tile] = o
            written.add(tile)
        if 1 <= t <= n:
            h = t - 1
            s = s_refs[h % 2][...]
            p = jnp.exp2(s - jnp.max(s, axis=-1, keepdims=True))
            denoms[h] = jnp.sum(p, axis=-1, keepdims=True)
            p_refs[h % 2][...] = p.astype(BF16)
        if t < n:
            s_refs[t % 2][...] = score_fn(*heads[t])


_HEAD_ORDER = [(tile, half) for tile in range(N_TILES) for half in range(2)]


def _group_norm_store(acc_ref, gain_ref, o_ref):
    tiles = [acc_ref[j] for j in range(N_TILES)]
    ss = sum(jnp.sum(o * o, axis=-1, keepdims=True) for o in tiles)
    inv = lax.rsqrt(ss * (1.0 / (N_TILES * LANES)) + EPS)
    for j in range(N_TILES):
        sl = slice(j * LANES, (j + 1) * LANES)
        o_ref[:, sl] = (tiles[j] * inv * gain_ref[:, sl]).astype(BF16)


_NT = (((1,), (1,)), ((), ()))


def _gqa_kernel(q_ref, k_ref, v_ref, gain_ref, o_ref, acc_ref, s0_ref, s1_ref, p0_ref, p1_ref):
    low = lax.broadcasted_iota(jnp.int32, (1, LANES), 1) < HEAD_DIM

    def scores(tile, half):
        qt = q_ref[tile]
        qm = jnp.where(jnp.logical_not(low) if half else low, qt, jnp.zeros_like(qt))
        return lax.dot_general(qm, k_ref[...], _NT, preferred_element_type=F32)

    _attention_pipeline(_HEAD_ORDER, scores, lambda tile: v_ref[...],
                        (s0_ref, s1_ref), (p0_ref, p1_ref), acc_ref, low)
    _group_norm_store(acc_ref, gain_ref, o_ref)


def _gqa(qa, ka, va, gain, batch, seq):
    t = batch * seq
    tq = TQ_GQA
    assert seq % tq == 0
    nq = seq // tq
    return pl.pallas_call(
        _gqa_kernel,
        grid=(batch, nq),
        in_specs=[
            pl.BlockSpec((N_TILES, tq, LANES), lambda b, i: (0, b * nq + i, 0)),
            pl.BlockSpec((seq, KV_WIDTH_A), lambda b, i: (b, 0)),
            pl.BlockSpec((seq, KV_WIDTH_A), lambda b, i: (b, 0)),
            _resident((1, WIDTH_A)),
        ],
        out_specs=pl.BlockSpec((tq, WIDTH_A), lambda b, i: (b * nq + i, 0)),
        out_shape=jax.ShapeDtypeStruct((t, WIDTH_A), BF16),
        scratch_shapes=[pltpu.VMEM((N_TILES, tq, LANES), F32),
                        pltpu.VMEM((tq, seq), F32), pltpu.VMEM((tq, seq), F32),
                        pltpu.VMEM((tq, seq), BF16), pltpu.VMEM((tq, seq), BF16)],
        compiler_params=_params(2),
        name="gqa",
    )(qa, ka, va, gain)


def _na_kernel(q_ref, k_ref, v_ref, bias_ref, gain_ref, o_ref, acc_ref, s0_ref, s1_ref, p0_ref, p1_ref,
               *, rows):
    g = pl.program_id(1)
    n_groups = rows // NA_QROWS
    w0 = jnp.clip(NA_QROWS * g - NA_KH // 2, 0, rows - NA_WROWS)
    start = pl.multiple_of(w0 * GRID_W, GRID_W)
    kind = jnp.where(g == 0, 0, jnp.where(g == n_groups - 1, 2, 1))
    low = lax.broadcasted_iota(jnp.int32, (1, LANES), 1) < HEAD_DIM
    window = pl.ds(start, NA_WROWS * GRID_W)

    def scores(tile, half):
        qt = q_ref[tile]
        qm = jnp.where(jnp.logical_not(low) if half else low, qt, jnp.zeros_like(qt))
        s = lax.dot_general(qm, k_ref[tile, window, :], _NT, preferred_element_type=F32)
        return s + bias_ref[kind, 2 * tile + half]

    _attention_pipeline(_HEAD_ORDER, scores, lambda tile: v_ref[tile, window, :],
                        (s0_ref, s1_ref), (p0_ref, p1_ref), acc_ref, low)
    _group_norm_store(acc_ref, gain_ref, o_ref)


def _na_bias_table(rpb):
    c = np.arange(GRID_W)
    c_start = np.clip(c - NA_KW // 2, 0, GRID_W - NA_KW)
    col_ok = (c[None, :] >= c_start[:, None]) & (c[None, :] < c_start[:, None] + NA_KW)
    rl = np.arange(NA_QROWS)[:, None]
    kl = np.arange(NA_WROWS)[None, :]
    half = NA_KH // 2
    assert NA_QROWS <= half
    first = (kl < NA_KH, kl - rl + NA_KH - 1)
    interior = ((kl - rl >= 0) & (kl - rl < NA_KH), kl - rl + NA_KH - 1 - half)
    last = (kl >= NA_WROWS - NA_KH, kl - rl - NA_WROWS + NA_QROWS + NA_KH - 1)
    row_ok = np.stack([np.broadcast_to(k[0], (NA_QROWS, NA_WROWS)) for k in (first, interior, last)])
    dr_idx = np.stack([np.clip(k[1], 0, 2 * NA_KH - 2) for k in (first, interior, last)])
    padded = jnp.pad(rpb.astype(F32), ((0, 0), (0, 0), (GRID_W, GRID_W)))
    lo = GRID_W + NA_KW - 1
    rpb_c = jnp.stack([padded[:, :, lo - q:lo - q + GRID_W] for q in range(GRID_W)], axis=2)
    rpb_c = jnp.where(col_ok[None, None], rpb_c * LOG2E, MASK_VALUE)
    masked = jnp.full((N_HEADS_B, GRID_W, GRID_W), MASK_VALUE, F32)
    kinds = []
    for kind in range(3):
        q_rows = []
        for r in range(NA_QROWS):
            blocks = [rpb_c[:, int(dr_idx[kind, r, k])] if row_ok[kind, r, k] else masked
                      for k in range(NA_WROWS)]
            q_rows.append(jnp.stack(blocks, axis=2))
        kinds.append(jnp.stack(q_rows, axis=1))
    tbl = jnp.stack(kinds, axis=0)
    return tbl.reshape(3, N_HEADS_B, NA_QROWS * GRID_W, NA_WROWS * GRID_W)


def _natten(qb, kb, vb, bias, gain, batch, seq):
    t = batch * seq
    rows = seq // GRID_W
    assert seq % GRID_W == 0 and rows % NA_QROWS == 0 and rows >= NA_WROWS
    n_groups = rows // NA_QROWS
    tq = NA_QROWS * GRID_W
    return pl.pallas_call(
        functools.partial(_na_kernel, rows=rows),
        grid=(batch, n_groups),
        in_specs=[
            pl.BlockSpec((N_TILES, tq, LANES), lambda b, g: (0, b * n_groups + g, 0)),
            pl.BlockSpec((N_TILES, seq, LANES), lambda b, g: (0, b, 0)),
            pl.BlockSpec((N_TILES, seq, LANES), lambda b, g: (0, b, 0)),
            _resident(bias.shape),
            _resident((1, WIDTH_B)),
        ],
        out_specs=pl.BlockSpec((tq, WIDTH_B), lambda b, g: (b * n_groups + g, 0)),
        out_shape=jax.ShapeDtypeStruct((t, WIDTH_B), BF16),
        scratch_shapes=[pltpu.VMEM((N_TILES, tq, LANES), F32),
                        pltpu.VMEM((tq, NA_WROWS * GRID_W), F32), pltpu.VMEM((tq, NA_WROWS * GRID_W), F32),
                        pltpu.VMEM((tq, NA_WROWS * GRID_W), BF16), pltpu.VMEM((tq, NA_WROWS * GRID_W), BF16)],
        compiler_params=_params(2),
        name="natten",
    )(qb, kb, vb, bias, gain)


def _out_proj_kernel(ma_ref, mb_ref, x_ref, w_ref, g_ref, h_ref, n_ref):
    mixed = jnp.concatenate([ma_ref[...], mb_ref[...]], axis=1)
    h = x_ref[...] + jnp.dot(mixed, w_ref[...], preferred_element_type=F32)
    h_ref[...] = h
    n_ref[...] = _rms(h, g_ref[...]).astype(BF16)


def _out_proj(mixed_a, mixed_b, x2, w_out, ffn_norm):
    t = x2.shape[0]
    tm = TM_PROJ
    row = lambda i: (i, 0)
    return pl.pallas_call(
        _out_proj_kernel,
        grid=(t // tm,),
        in_specs=[
            pl.BlockSpec((tm, WIDTH_A), row),
            pl.BlockSpec((tm, WIDTH_B), row),
            pl.BlockSpec((tm, D_MODEL), row),
            _resident((WIDTH_A + WIDTH_B, D_MODEL)),
            _resident((1, D_MODEL)),
        ],
        out_specs=[pl.BlockSpec((tm, D_MODEL), row), pl.BlockSpec((tm, D_MODEL), row)],
        out_shape=[jax.ShapeDtypeStruct((t, D_MODEL), F32), jax.ShapeDtypeStruct((t, D_MODEL), BF16)],
        compiler_params=_params(1),
        name="out_proj",
    )(mixed_a, mixed_b, x2, w_out, ffn_norm)


def _gelu_tanh(x):
    k = -2.0 * math.sqrt(2.0 / math.pi) * math.log2(math.e)
    return x / (1.0 + jnp.exp2(x * (k + (k * 0.044715) * (x * x))))


def _ffn_kernel(n_ref, prev_ref, next_ref, h_ref, p_ref, wup_ref, cw_ref, cb_ref, wdn_ref,
                pn_ref, wg_ref, wp_ref, fn_ref, y_ref, ext_ref, u_ref, act_ref, acc_ref,
                *, tiles_per_seq):
    i = pl.program_id(0)
    tm = n_ref.shape[0]
    rows = tm + 2 * HALO
    pos = i % tiles_per_seq
    prev = prev_ref[...]
    nxt = next_ref[...]
    ext_ref[0:HALO] = jnp.where(pos != 0, prev, jnp.zeros_like(prev))
    ext_ref[HALO:HALO + tm] = n_ref[...]
    ext_ref[HALO + tm:] = jnp.where(pos != tiles_per_seq - 1, nxt, jnp.zeros_like(nxt))

    for t in range(N_FF_CHUNKS + 2):
        if t < N_FF_CHUNKS:
            u_ref[t % 2] = jnp.dot(ext_ref[...], wup_ref[t], preferred_element_type=F32)
        if 1 <= t <= N_FF_CHUNKS:
            c = t - 1
            cw = cw_ref[c]
            u = u_ref[c % 2]
            conv = (pltpu.roll(u, 1, axis=0)[HALO:HALO + tm] * cw[0:1]
                    + u[HALO:HALO + tm] * cw[1:2]
                    + pltpu.roll(u, rows - 1, axis=0)[HALO:HALO + tm] * cw[2:3]
                    + cb_ref[c])
            act = conv[:, :FF_CHUNK] * _gelu_tanh(conv[:, FF_CHUNK:])
            act_ref[c % 2] = act.astype(BF16)
        if t >= 2:
            c = t - 2
            d = jnp.dot(act_ref[c % 2], wdn_ref[c], preferred_element_type=F32)
            if c == 0:
                acc_ref[...] = d
            else:
                acc_ref[...] += d


    h = h_ref[...] + acc_ref[...]
    gate = jax.nn.sigmoid(jnp.dot(_rms(h, pn_ref[...]).astype(BF16), wg_ref[...],
                                  preferred_element_type=F32))
    emb = jnp.dot(p_ref[...].astype(BF16), wp_ref[...], preferred_element_type=F32)
    h = h + gate * emb
    y_ref[...] = _rms(h, fn_ref[...])


def _ffn(n2, h1, p2, w_up, conv_w, conv_b, w_down, ple_norm, w_gate, w_ple, final_norm, seq):
    t = n2.shape[0]
    tm = TM_FFN
    assert t % tm == 0 and seq % tm == 0 and tm % HALO == 0
    tiles_per_seq = seq // tm
    halo_per_tile = tm // HALO
    n_halo_blocks = t // HALO
    row = lambda i: (i, 0)
    return pl.pallas_call(
        functools.partial(_ffn_kernel, tiles_per_seq=tiles_per_seq),
        grid=(t // tm,),
        in_specs=[
            pl.BlockSpec((tm, D_MODEL), row),
            pl.BlockSpec((HALO, D_MODEL), lambda i: (jnp.maximum(i * halo_per_tile - 1, 0), 0)),
            pl.BlockSpec((HALO, D_MODEL),
                         lambda i: (jnp.minimum((i + 1) * halo_per_tile, n_halo_blocks - 1), 0)),
            pl.BlockSpec((tm, D_MODEL), row),
            pl.BlockSpec((tm, PLE_DIM), row),
            _resident((N_FF_CHUNKS, D_MODEL, 2 * FF_CHUNK)),
            _resident((N_FF_CHUNKS, 3, 2 * FF_CHUNK)),
            _resident((N_FF_CHUNKS, 1, 2 * FF_CHUNK)),
            _resident((N_FF_CHUNKS, FF_CHUNK, D_MODEL)),
            _resident((1, D_MODEL)),
            _resident((D_MODEL, D_MODEL)),
            _resident((PLE_DIM, D_MODEL)),
            _resident((1, D_MODEL)),
        ],
        out_specs=pl.BlockSpec((tm, D_MODEL), row),
        out_shape=jax.ShapeDtypeStruct((t, D_MODEL), F32),
        scratch_shapes=[pltpu.VMEM((tm + 2 * HALO, D_MODEL), BF16),
                        pltpu.VMEM((2, tm + 2 * HALO, 2 * FF_CHUNK), F32),
                        pltpu.VMEM((2, tm, FF_CHUNK), BF16),
                        pltpu.VMEM((tm, D_MODEL), F32)],
        compiler_params=_params(1),
        name="ffn",
    )(n2, n2, n2, h1, p2, w_up, conv_w, conv_b, w_down, ple_norm, w_gate, w_ple, final_norm)


def _pair_heads(w, axis):
    group = N_Q_A // N_KV_A
    shape = w.shape
    split = shape[:axis] + (N_KV_A, group, HEAD_DIM) + shape[axis + 1:]
    return jnp.swapaxes(w.reshape(split), axis, axis + 1).reshape(shape)


def _rope_tables(seq):
    t = jnp.arange(seq)
    row = (t // GRID_W).astype(F32)
    col = (t % GRID_W).astype(F32)
    freqs = ROPE_THETA ** (-jnp.arange(0, ROPE_AXIS_DIM, 2, dtype=F32) / ROPE_AXIS_DIM)
    ang_r = row[:, None] * freqs[None]
    ang_c = col[:, None] * freqs[None]
    cos_h = jnp.concatenate([jnp.cos(ang_r)] * 2 + [jnp.cos(ang_c)] * 2, axis=1)
    sin_h = jnp.concatenate([-jnp.sin(ang_r), jnp.sin(ang_r), -jnp.sin(ang_c), jnp.sin(ang_c)], axis=1)
    return jnp.concatenate([cos_h, cos_h], axis=1), jnp.concatenate([sin_h, sin_h], axis=1)


def _prepare_params(attn_norm, w_in, q_norm_a, k_norm_a, rpb_b, out_norm_a, out_norm_b, w_out,
                    ffn_norm, w_up, conv_w, conv_b, w_down, ple_norm, w_ple_gate, w_ple, final_norm):
    head_id = np.arange(WIDTH_A) // HEAD_DIM
    head_mean = jnp.asarray((head_id[:, None] == head_id[None, :]) / HEAD_DIM, BF16)
    row2 = lambda v: v.reshape(1, -1).astype(F32)
    w_in_b = w_in[0].astype(BF16)
    w_out_b = w_out[0].astype(BF16)

    def chunked(w):
        r = w.shape[0]
        return w.reshape(r, 2, N_FF_CHUNKS, FF_CHUNK).transpose(2, 0, 1, 3).reshape(
            N_FF_CHUNKS, r, 2 * FF_CHUNK)

    return dict(
        attn_norm=row2(attn_norm[0]),
        w_in=jnp.concatenate([_pair_heads(w_in_b[:, :WIDTH_A], 1), w_in_b[:, WIDTH_A:]], axis=1),
        q_gain=row2(jnp.tile(q_norm_a[0], N_Q_A)),
        k_gain=row2(jnp.tile(k_norm_a[0], N_KV_A)),
        head_mean=head_mean,
        na_bias=_na_bias_table(rpb_b[0]),
        gain_a=row2(_pair_heads(out_norm_a[0], 0)),
        gain_b=row2(out_norm_b[0]),
        w_out=jnp.concatenate([_pair_heads(w_out_b[:WIDTH_A], 0), w_out_b[WIDTH_A:]], axis=0),
        ffn_norm=row2(ffn_norm[0]),
        w_up=chunked(w_up[0].astype(BF16)),
        conv_w=chunked(conv_w[0].astype(F32)),
        conv_b=chunked(conv_b[0].astype(F32)[None]),
        w_down=w_down[0].astype(BF16).reshape(N_FF_CHUNKS, FF_CHUNK, D_MODEL),
        ple_norm=row2(ple_norm[0]),
        w_gate=w_ple_gate[0].astype(BF16),
        w_ple=w_ple[0].astype(BF16),
        final_norm=row2(final_norm),
    )


def _trunk(x, p, w):
    batch, seq, _ = x.shape
    t = batch * seq
    x2 = x.reshape(t, D_MODEL)
    p2 = p[0].reshape(t, PLE_DIM)
    cos_t, sin_t = _rope_tables(seq)
    qa, ka, va, qb, kb, vb = _in_proj(x2, w["attn_norm"], w["w_in"], w["q_gain"], w["k_gain"],
                                      cos_t, sin_t, w["head_mean"], seq)
    mixed_a = _gqa(qa, ka, va, w["gain_a"], batch, seq)
    mixed_b = _natten(qb, kb, vb, w["na_bias"], w["gain_b"], batch, seq)
    h1, n2 = _out_proj(mixed_a, mixed_b, x2, w["w_out"], w["ffn_norm"])
    y = _ffn(n2, h1, p2, w["w_up"], w["conv_w"], w["conv_b"], w["w_down"], w["ple_norm"],
             w["w_gate"], w["w_ple"], w["final_norm"], seq)
    return y.reshape(batch, seq, D_MODEL)


def kernel(x_prompt, x_sample, p_prompt, p_sample, attn_norm, w_in, q_norm_a, k_norm_a, rpb_b,
           out_norm_a, out_norm_b, w_out, ffn_norm, w_up, conv_w, conv_b, w_down, ple_norm,
           w_ple_gate, w_ple, final_norm):
    w = _prepare_params(attn_norm, w_in, q_norm_a, k_norm_a, rpb_b, out_norm_a, out_norm_b, w_out,
                        ffn_norm, w_up, conv_w, conv_b, w_down, ple_norm, w_ple_gate, w_ple, final_norm)
    return (_trunk(x_prompt, p_prompt, w), _trunk(x_sample, p_sample, w))
```

```python
import functools
import math

import numpy as np
import jax
import jax.numpy as jnp
from jax import lax
from jax.experimental import pallas as pl
from jax.experimental.pallas import tpu as pltpu

F32 = jnp.float32
BF16 = jnp.bfloat16

D_MODEL = 1024
HEAD_DIM = 64
N_Q_A = 8
N_KV_A = 2
N_HEADS_B = 8
WIDTH_A = N_Q_A * HEAD_DIM
WIDTH_B = N_HEADS_B * HEAD_DIM
KV_WIDTH_A = N_KV_A * HEAD_DIM
IN_PROJ_WIDTH = WIDTH_A + 2 * KV_WIDTH_A + 3 * WIDTH_B
GRID_W = 64
ROPE_AXIS_DIM = HEAD_DIM // 2
ROPE_HALF = ROPE_AXIS_DIM // 2
ROPE_THETA = 10000.0
NA_KH = 8
NA_KW = 16
D_FF = 2816
PLE_DIM = 256
EPS = 1e-6
LOG2E = math.log2(math.e)
Q_SCALE = LOG2E / math.sqrt(HEAD_DIM)

LANES = 128
BF16_SUBLANES = 16
VMEM_LIMIT_BYTES = 56 * 1024 * 1024

N_TILES = WIDTH_A // LANES
MASK_VALUE = -1e30

TM_PROJ = 1024
TQ_GQA = 512
NA_QROWS = 4
NA_WROWS = NA_QROWS + NA_KH
TM_FFN = 512
FF_CHUNK = 256
N_FF_CHUNKS = D_FF // FF_CHUNK
HALO = BF16_SUBLANES


def _rms(x, gain):
    ms = jnp.mean(x * x, axis=-1, keepdims=True)
    return x * lax.rsqrt(ms + EPS) * gain


def _resident(shape):
    zeros = (0,) * len(shape)
    return pl.BlockSpec(shape, lambda *_: zeros, pipeline_mode=pl.Buffered(1))


def _params(n_axes):
    return pltpu.CompilerParams(
        dimension_semantics=("arbitrary",) * n_axes,
        vmem_limit_bytes=VMEM_LIMIT_BYTES,
    )


def _in_proj_kernel(x_ref, g_ref, w_ref, qg_ref, kg_ref, cos_ref, sin_ref, hm_ref,
                    qa_ref, ka_ref, va_ref, qb_ref, kb_ref, vb_ref):
    n = _rms(x_ref[...], g_ref[...]).astype(BF16)
    proj = jnp.dot(n, w_ref[...], preferred_element_type=F32)
    cos = cos_ref[...]
    sin = sin_ref[...]
    lane = lax.broadcasted_iota(jnp.int32, (1, LANES), 1)
    first_half = (lane & ROPE_HALF) == 0
    head_mean = hm_ref[...]

    def head_norm(t, gain):
        w = t.shape[1]
        ms = jnp.dot((t * t).astype(BF16), head_mean[:w, :w], preferred_element_type=F32)
        return t * lax.rsqrt(ms + EPS) * gain

    def rope(t):
        ahead = pltpu.roll(t, LANES - ROPE_HALF, axis=1)
        behind = pltpu.roll(t, ROPE_HALF, axis=1)
        return t * cos + jnp.where(first_half, ahead, behind) * sin

    qa = head_norm(proj[:, :WIDTH_A], qg_ref[...])
    for j in range(N_TILES):
        qa_ref[j] = (rope(qa[:, j * LANES:(j + 1) * LANES]) * Q_SCALE).astype(BF16)
    off = WIDTH_A
    ka_ref[...] = rope(head_norm(proj[:, off:off + KV_WIDTH_A], kg_ref[...])).astype(BF16)
    off += KV_WIDTH_A
    va_ref[...] = proj[:, off:off + KV_WIDTH_A].astype(BF16)
    off += KV_WIDTH_A
    for j in range(N_TILES):
        lo = off + j * LANES
        qb_ref[j] = (proj[:, lo:lo + LANES] * Q_SCALE).astype(BF16)
        kb_ref[j] = proj[:, lo + WIDTH_B:lo + WIDTH_B + LANES].astype(BF16)
        vb_ref[j] = proj[:, lo + 2 * WIDTH_B:lo + 2 * WIDTH_B + LANES].astype(BF16)


def _in_proj(x2, attn_norm, w_in, q_gain, k_gain, cos_t, sin_t, head_mean, seq):
    t = x2.shape[0]
    tm = TM_PROJ
    assert t % tm == 0 and seq % tm == 0
    tiles_per_seq = seq // tm
    row = lambda i: (i, 0)
    tile3 = lambda i: (0, i, 0)
    pos = lambda i: (i % tiles_per_seq, 0)
    tiled = jax.ShapeDtypeStruct((N_TILES, t, LANES), BF16)
    flat = jax.ShapeDtypeStruct((t, KV_WIDTH_A), BF16)
    return pl.pallas_call(
        _in_proj_kernel,
        grid=(t // tm,),
        in_specs=[
            pl.BlockSpec((tm, D_MODEL), row),
            _resident((1, D_MODEL)),
            _resident((D_MODEL, IN_PROJ_WIDTH)),
            _resident((1, WIDTH_A)),
            _resident((1, KV_WIDTH_A)),
            pl.BlockSpec((tm, LANES), pos),
            pl.BlockSpec((tm, LANES), pos),
            _resident((WIDTH_A, WIDTH_A)),
        ],
        out_specs=[
            pl.BlockSpec((N_TILES, tm, LANES), tile3),
            pl.BlockSpec((tm, KV_WIDTH_A), row),
            pl.BlockSpec((tm, KV_WIDTH_A), row),
            pl.BlockSpec((N_TILES, tm, LANES), tile3),
            pl.BlockSpec((N_TILES, tm, LANES), tile3),
            pl.BlockSpec((N_TILES, tm, LANES), tile3),
        ],
        out_shape=[tiled, flat, flat, tiled, tiled, tiled],
        compiler_params=_params(1),
        name="in_proj",
    )(x2, attn_norm, w_in, q_gain, k_gain, cos_t, sin_t, head_mean)


def _attention_pipeline(heads, score_fn, value_fn, s_refs, p_refs, acc_ref, low):
    n = len(heads)
    denoms = {}
    written = set()
    for t in range(n + 2):
        if t >= 2:
            h = t - 2
            tile, half = heads[h]
            o = jnp.dot(p_refs[h % 2][...], value_fn(tile), preferred_element_type=F32) / denoms.pop(h)
            if tile in written:
                prev = acc_ref[tile]
                o = jnp.where(low, prev, o) if half else jnp.where(low, o, prev)
            acc_ref[tile] = o
            written.add(tile)
        if 1 <= t <= n:
            h = t - 1
            s = s_refs[h % 2][...]
            p = jnp.exp2(s - jnp.max(s, axis=-1, keepdims=True))
            denoms[h] = jnp.sum(p, axis=-1, keepdims=True)
            p_refs[h % 2][...] = p.astype(BF16)
        if t < n:
            s_refs[t % 2][...] = score_fn(*heads[t])


_HEAD_ORDER = [(tile, half) for tile in range(N_TILES) for half in range(2)]


def _group_norm_store(acc_ref, gain_ref, o_ref):
    tiles = [acc_ref[j] for j in range(N_TILES)]
    ss = sum(jnp.sum(o * o, axis=-1, keepdims=True) for o in tiles)
    inv = lax.rsqrt(ss * (1.0 / (N_TILES * LANES)) + EPS)
    for j in range(N_TILES):
        sl = slice(j * LANES, (j + 1) * LANES)
        o_ref[:, sl] = (tiles[j] * inv * gain_ref[:, sl]).astype(BF16)


_NT = (((1,), (1,)), ((), ()))


def _gqa_kernel(q_ref, k_ref, v_ref, gain_ref, o_ref, acc_ref, s0_ref, s1_ref, p0_ref, p1_ref):
    low = lax.broadcasted_iota(jnp.int32, (1, LANES), 1) < HEAD_DIM

    def scores(tile, half):
        qt = q_ref[tile]
        qm = jnp.where(jnp.logical_not(low) if half else low, qt, jnp.zeros_like(qt))
        return lax.dot_general(qm, k_ref[...], _NT, preferred_element_type=F32)

    _attention_pipeline(_HEAD_ORDER, scores, lambda tile: v_ref[...],
                        (s0_ref, s1_ref), (p0_ref, p1_ref), acc_ref, low)
    _group_norm_store(acc_ref, gain_ref, o_ref)


def _gqa(qa, ka, va, gain, batch, seq):
    t = batch * seq
    tq = TQ_GQA
    assert seq % tq == 0
    nq = seq // tq
    return pl.pallas_call(
        _gqa_kernel,
        grid=(batch, nq),
        in_specs=[
            pl.BlockSpec((N_TILES, tq, LANES), lambda b, i: (0, b * nq + i, 0)),
            pl.BlockSpec((seq, KV_WIDTH_A), lambda b, i: (b, 0)),
            pl.BlockSpec((seq, KV_WIDTH_A), lambda b, i: (b, 0)),
            _resident((1, WIDTH_A)),
        ],
        out_specs=pl.BlockSpec((tq, WIDTH_A), lambda b, i: (b * nq + i, 0)),
        out_shape=jax.ShapeDtypeStruct((t, WIDTH_A), BF16),
        scratch_shapes=[pltpu.VMEM((N_TILES, tq, LANES), F32),
                        pltpu.VMEM((tq, seq), F32), pltpu.VMEM((tq, seq), F32),
                        pltpu.VMEM((tq, seq), BF16), pltpu.VMEM((tq, seq), BF16)],
        compiler_params=_params(2),
        name="gqa",
    )(qa, ka, va, gain)


def _na_kernel(q_ref, k_ref, v_ref, bias_ref, gain_ref, o_ref, acc_ref, s0_ref, s1_ref, p0_ref, p1_ref,
               *, rows):
    g = pl.program_id(1)
    n_groups = rows // NA_QROWS
    w0 = jnp.clip(NA_QROWS * g - NA_KH // 2, 0, rows - NA_WROWS)
    start = pl.multiple_of(w0 * GRID_W, GRID_W)
    kind = jnp.where(g == 0, 0, jnp.where(g == n_groups - 1, 2, 1))
    low = lax.broadcasted_iota(jnp.int32, (1, LANES), 1) < HEAD_DIM
    window = pl.ds(start, NA_WROWS * GRID_W)

    def scores(tile, half):
        qt = q_ref[tile]
        qm = jnp.where(jnp.logical_not(low) if half else low, qt, jnp.zeros_like(qt))
        s = lax.dot_general(qm, k_ref[tile, window, :], _NT, preferred_element_type=F32)
        return s + bias_ref[kind, 2 * tile + half]

    _attention_pipeline(_HEAD_ORDER, scores, lambda tile: v_ref[tile, window, :],
                        (s0_ref, s1_ref), (p0_ref, p1_ref), acc_ref, low)
    _group_norm_store(acc_ref, gain_ref, o_ref)


def _na_bias_table(rpb):
    c = np.arange(GRID_W)
    c_start = np.clip(c - NA_KW // 2, 0, GRID_W - NA_KW)
    col_ok = (c[None, :] >= c_start[:, None]) & (c[None, :] < c_start[:, None] + NA_KW)
    rl = np.arange(NA_QROWS)[:, None]
    kl = np.arange(NA_WROWS)[None, :]
    half = NA_KH // 2
    assert NA_QROWS <= half
    first = (kl < NA_KH, kl - rl + NA_KH - 1)
    interior = ((kl - rl >= 0) & (kl - rl < NA_KH), kl - rl + NA_KH - 1 - half)
    last = (kl >= NA_WROWS - NA_KH, kl - rl - NA_WROWS + NA_QROWS + NA_KH - 1)
    row_ok = np.stack([np.broadcast_to(k[0], (NA_QROWS, NA_WROWS)) for k in (first, interior, last)])
    dr_idx = np.stack([np.clip(k[1], 0, 2 * NA_KH - 2) for k in (first, interior, last)])
    padded = jnp.pad(rpb.astype(F32), ((0, 0), (0, 0), (GRID_W, GRID_W)))
    lo = GRID_W + NA_KW - 1
    rpb_c = jnp.stack([padded[:, :, lo - q:lo - q + GRID_W] for q in range(GRID_W)], axis=2)
    rpb_c = jnp.where(col_ok[None, None], rpb_c * LOG2E, MASK_VALUE)
    masked = jnp.full((N_HEADS_B, GRID_W, GRID_W), MASK_VALUE, F32)
    kinds = []
    for kind in range(3):
        q_rows = []
        for r in range(NA_QROWS):
            blocks = [rpb_c[:, int(dr_idx[kind, r, k])] if row_ok[kind, r, k] else masked
                      for k in range(NA_WROWS)]
            q_rows.append(jnp.concatenate(blocks, axis=2))
        kinds.append(jnp.concatenate(q_rows, axis=1))
    return jnp.stack(kinds, axis=0)


def _natten(qb, kb, vb, bias, gain, batch, seq):
    t = batch * seq
    rows = seq // GRID_W
    assert seq % GRID_W == 0 and rows % NA_QROWS == 0 and rows >= NA_WROWS
    n_groups = rows // NA_QROWS
    tq = NA_QROWS * GRID_W
    return pl.pallas_call(
        functools.partial(_na_kernel, rows=rows),
        grid=(batch, n_groups),
        in_specs=[
            pl.BlockSpec((N_TILES, tq, LANES), lambda b, g: (0, b * n_groups + g, 0)),
            pl.BlockSpec((N_TILES, seq, LANES), lambda b, g: (0, b, 0)),
            pl.BlockSpec((N_TILES, seq, LANES), lambda b, g: (0, b, 0)),
            _resident(bias.shape),
            _resident((1, WIDTH_B)),
        ],
        out_specs=pl.BlockSpec((tq, WIDTH_B), lambda b, g: (b * n_groups + g, 0)),
        out_shape=jax.ShapeDtypeStruct((t, WIDTH_B), BF16),
        scratch_shapes=[pltpu.VMEM((N_TILES, tq, LANES), F32),
                        pltpu.VMEM((tq, NA_WROWS * GRID_W), F32), pltpu.VMEM((tq, NA_WROWS * GRID_W), F32),
                        pltpu.VMEM((tq, NA_WROWS * GRID_W), BF16), pltpu.VMEM((tq, NA_WROWS * GRID_W), BF16)],
        compiler_params=_params(2),
        name="natten",
    )(qb, kb, vb, bias, gain)


def _out_proj_kernel(ma_ref, mb_ref, x_ref, w_ref, g_ref, h_ref, n_ref):
    mixed = jnp.concatenate([ma_ref[...], mb_ref[...]], axis=1)
    h = x_ref[...] + jnp.dot(mixed, w_ref[...], preferred_element_type=F32)
    h_ref[...] = h
    n_ref[...] = _rms(h, g_ref[...]).astype(BF16)


def _out_proj(mixed_a, mixed_b, x2, w_out, ffn_norm):
    t = x2.shape[0]
    tm = TM_PROJ
    row = lambda i: (i, 0)
    return pl.pallas_call(
        _out_proj_kernel,
        grid=(t // tm,),
        in_specs=[
            pl.BlockSpec((tm, WIDTH_A), row),
            pl.BlockSpec((tm, WIDTH_B), row),
            pl.BlockSpec((tm, D_MODEL), row),
            _resident((WIDTH_A + WIDTH_B, D_MODEL)),
            _resident((1, D_MODEL)),
        ],
        out_specs=[pl.BlockSpec((tm, D_MODEL), row), pl.BlockSpec((tm, D_MODEL), row)],
        out_shape=[jax.ShapeDtypeStruct((t, D_MODEL), F32), jax.ShapeDtypeStruct((t, D_MODEL), BF16)],
        compiler_params=_params(1),
        name="out_proj",
    )(mixed_a, mixed_b, x2, w_out, ffn_norm)


def _gelu_tanh(x):
    k = -2.0 * math.sqrt(2.0 / math.pi) * math.log2(math.e)
    return x / (1.0 + jnp.exp2(x * (k + (k * 0.044715) * (x * x))))


def _ffn_kernel(n_ref, prev_ref, next_ref, h_ref, p_ref, wup_ref, cw_ref, cb_ref, wdn_ref,
                pn_ref, wg_ref, wp_ref, fn_ref, y_ref, ext_ref, u_ref, act_ref, acc_ref,
                *, tiles_per_seq):
    i = pl.program_id(0)
    tm = n_ref.shape[0]
    pos = i % tiles_per_seq
    prev = prev_ref[...]
    nxt = next_ref[...]
    ext_ref[0:HALO] = jnp.where(pos != 0, prev, jnp.zeros_like(prev))
    ext_ref[HALO:HALO + tm] = n_ref[...]
    ext_ref[HALO + tm:] = jnp.where(pos != tiles_per_seq - 1, nxt, jnp.zeros_like(nxt))

    for t in range(N_FF_CHUNKS + 2):
        if t < N_FF_CHUNKS:
            u_ref[t % 2] = jnp.dot(ext_ref[...], wup_ref[t], preferred_element_type=F32)
        if 1 <= t <= N_FF_CHUNKS:
            c = t - 1
            cw = cw_ref[c]
            conv = (u_ref[c % 2, HALO - 1:HALO - 1 + tm] * cw[0:1]
                    + u_ref[c % 2, HALO:HALO + tm] * cw[1:2]
                    + u_ref[c % 2, HALO + 1:HALO + 1 + tm] * cw[2:3]
                    + cb_ref[c])
            act = conv[:, :FF_CHUNK] * _gelu_tanh(conv[:, FF_CHUNK:])
            act_ref[c % 2] = act.astype(BF16)
        if t >= 2:
            c = t - 2
            d = jnp.dot(act_ref[c % 2], wdn_ref[c], preferred_element_type=F32)
            if c == 0:
                acc_ref[...] = d
            else:
                acc_ref[...] += d


    h = h_ref[...] + acc_ref[...]
    gate = jax.nn.sigmoid(jnp.dot(_rms(h, pn_ref[...]).astype(BF16), wg_ref[...],
                                  preferred_element_type=F32))
    emb = jnp.dot(p_ref[...].astype(BF16), wp_ref[...], preferred_element_type=F32)
    h = h + gate * emb
    y_ref[...] = _rms(h, fn_ref[...])


def _ffn(n2, h1, p2, w_up, conv_w, conv_b, w_down, ple_norm, w_gate, w_ple, final_norm, seq):
    t = n2.shape[0]
    tm = TM_FFN
    assert t % tm == 0 and seq % tm == 0 and tm % HALO == 0
    tiles_per_seq = seq // tm
    halo_per_tile = tm // HALO
    n_halo_blocks = t // HALO
    row = lambda i: (i, 0)
    return pl.pallas_call(
        functools.partial(_ffn_kernel, tiles_per_seq=tiles_per_seq),
        grid=(t // tm,),
        in_specs=[
            pl.BlockSpec((tm, D_MODEL), row),
            pl.BlockSpec((HALO, D_MODEL), lambda i: (jnp.maximum(i * halo_per_tile - 1, 0), 0)),
            pl.BlockSpec((HALO, D_MODEL),
                         lambda i: (jnp.minimum((i + 1) * halo_per_tile, n_halo_blocks - 1), 0)),
            pl.BlockSpec((tm, D_MODEL), row),
            pl.BlockSpec((tm, PLE_DIM), row),
            _resident((N_FF_CHUNKS, D_MODEL, 2 * FF_CHUNK)),
            _resident((N_FF_CHUNKS, 3, 2 * FF_CHUNK)),
            _resident((N_FF_CHUNKS, 1, 2 * FF_CHUNK)),
            _resident((N_FF_CHUNKS, FF_CHUNK, D_MODEL)),
            _resident((1, D_MODEL)),
            _resident((D_MODEL, D_MODEL)),
            _resident((PLE_DIM, D_MODEL)),
            _resident((1, D_MODEL)),
        ],
        out_specs=pl.BlockSpec((tm, D_MODEL), row),
        out_shape=jax.ShapeDtypeStruct((t, D_MODEL), F32),
        scratch_shapes=[pltpu.VMEM((tm + 2 * HALO, D_MODEL), BF16),
                        pltpu.VMEM((2, tm + 2 * HALO, 2 * FF_CHUNK), F32),
                        pltpu.VMEM((2, tm, FF_CHUNK), BF16),
                        pltpu.VMEM((tm, D_MODEL), F32)],
        compiler_params=_params(1),
        name="ffn",
    )(n2, n2, n2, h1, p2, w_up, conv_w, conv_b, w_down, ple_norm, w_gate, w_ple, final_norm)


def _pair_heads(w, axis):
    group = N_Q_A // N_KV_A
    shape = w.shape
    split = shape[:axis] + (N_KV_A, group, HEAD_DIM) + shape[axis + 1:]
    return jnp.swapaxes(w.reshape(split), axis, axis + 1).reshape(shape)


def _rope_tables(seq):
    t = jnp.arange(seq)
    row = (t // GRID_W).astype(F32)
    col = (t % GRID_W).astype(F32)
    freqs = ROPE_THETA ** (-jnp.arange(0, ROPE_AXIS_DIM, 2, dtype=F32) / ROPE_AXIS_DIM)
    ang_r = row[:, None] * freqs[None]
    ang_c = col[:, None] * freqs[None]
    cos_h = jnp.concatenate([jnp.cos(ang_r)] * 2 + [jnp.cos(ang_c)] * 2, axis=1)
    sin_h = jnp.concatenate([-jnp.sin(ang_r), jnp.sin(ang_r), -jnp.sin(ang_c), jnp.sin(ang_c)], axis=1)
    return jnp.concatenate([cos_h, cos_h], axis=1), jnp.concatenate([sin_h, sin_h], axis=1)


def _prepare_params(attn_norm, w_in, q_norm_a, k_norm_a, rpb_b, out_norm_a, out_norm_b, w_out,
                    ffn_norm, w_up, conv_w, conv_b, w_down, ple_norm, w_ple_gate, w_ple, final_norm):
    head_id = np.arange(WIDTH_A) // HEAD_DIM
    head_mean = jnp.asarray((head_id[:, None] == head_id[None, :]) / HEAD_DIM, BF16)
    row2 = lambda v: v.reshape(1, -1).astype(F32)
    w_in_b = w_in[0].astype(BF16)
    w_out_b = w_out[0].astype(BF16)

    def chunked(w):
        r = w.shape[0]
        return w.reshape(r, 2, N_FF_CHUNKS, FF_CHUNK).transpose(2, 0, 1, 3).reshape(
            N_FF_CHUNKS, r, 2 * FF_CHUNK)

    return dict(
        attn_norm=row2(attn_norm[0]),
        w_in=jnp.concatenate([_pair_heads(w_in_b[:, :WIDTH_A], 1), w_in_b[:, WIDTH_A:]], axis=1),
        q_gain=row2(jnp.tile(q_norm_a[0], N_Q_A)),
        k_gain=row2(jnp.tile(k_norm_a[0], N_KV_A)),
        head_mean=head_mean,
        na_bias=_na_bias_table(rpb_b[0]),
        gain_a=row2(_pair_heads(out_norm_a[0], 0)),
        gain_b=row2(out_norm_b[0]),
        w_out=jnp.concatenate([_pair_heads(w_out_b[:WIDTH_A], 0), w_out_b[WIDTH_A:]], axis=0),
        ffn_norm=row2(ffn_norm[0]),
        w_up=chunked(w_up[0].astype(BF16)),
        conv_w=chunked(conv_w[0].astype(F32)),
        conv_b=chunked(conv_b[0].astype(F32)[None]),
        w_down=w_down[0].astype(BF16).reshape(N_FF_CHUNKS, FF_CHUNK, D_MODEL),
        ple_norm=row2(ple_norm[0]),
        w_gate=w_ple_gate[0].astype(BF16),
        w_ple=w_ple[0].astype(BF16),
        final_norm=row2(final_norm),
    )


def _trunk(x, p, w):
    batch, seq, _ = x.shape
    t = batch * seq
    x2 = x.reshape(t, D_MODEL)
    p2 = p[0].reshape(t, PLE_DIM)
    cos_t, sin_t = _rope_tables(seq)
    qa, ka, va, qb, kb, vb = _in_proj(x2, w["attn_norm"], w["w_in"], w["q_gain"], w["k_gain"],
                                      cos_t, sin_t, w["head_mean"], seq)
    mixed_a = _gqa(qa, ka, va, w["gain_a"], batch, seq)
    mixed_b = _natten(qb, kb, vb, w["na_bias"], w["gain_b"], batch, seq)
    h1, n2 = _out_proj(mixed_a, mixed_b, x2, w["w_out"], w["ffn_norm"])
    y = _ffn(n2, h1, p2, w["w_up"], w["conv_w"], w["conv_b"], w["w_down"], w["ple_norm"],
             w["w_gate"], w["w_ple"], w["final_norm"], seq)
    return y.reshape(batch, seq, D_MODEL)


def kernel(x_prompt, x_sample, p_prompt, p_sample, attn_norm, w_in, q_norm_a, k_norm_a, rpb_b,
           out_norm_a, out_norm_b, w_out, ffn_norm, w_up, conv_w, conv_b, w_down, ple_norm,
           w_ple_gate, w_ple, final_norm):
    w = _prepare_params(attn_norm, w_in, q_norm_a, k_norm_a, rpb_b, out_norm_a, out_norm_b, w_out,
                        ffn_norm, w_up, conv_w, conv_b, w_down, ple_norm, w_ple_gate, w_ple, final_norm)
    return (_trunk(x_prompt, p_prompt, w), _trunk(x_sample, p_sample, w))
```

```python
import functools
import math

import numpy as np
import jax
import jax.numpy as jnp
from jax import lax
from jax.experimental import pallas as pl
from jax.experimental.pallas import tpu as pltpu

F32 = jnp.float32
BF16 = jnp.bfloat16

D_MODEL = 1024
HEAD_DIM = 64
N_Q_A = 8
N_KV_A = 2
N_HEADS_B = 8
WIDTH_A = N_Q_A * HEAD_DIM
WIDTH_B = N_HEADS_B * HEAD_DIM
KV_WIDTH_A = N_KV_A * HEAD_DIM
IN_PROJ_WIDTH = WIDTH_A + 2 * KV_WIDTH_A + 3 * WIDTH_B
GRID_W = 64
ROPE_AXIS_DIM = HEAD_DIM // 2
ROPE_HALF = ROPE_AXIS_DIM // 2
ROPE_THETA = 10000.0
NA_KH = 8
NA_KW = 16
D_FF = 2816
PLE_DIM = 256
EPS = 1e-6
LOG2E = math.log2(math.e)
Q_SCALE = LOG2E / math.sqrt(HEAD_DIM)

LANES = 128
BF16_SUBLANES = 16
VMEM_LIMIT_BYTES = 56 * 1024 * 1024

N_TILES = WIDTH_A // LANES
MASK_VALUE = -1e30

TM_PROJ = 1024
TQ_GQA = 512
NA_QROWS = 4
NA_WROWS = NA_QROWS + NA_KH
TM_FFN = 512
FF_CHUNK = 256
N_FF_CHUNKS = D_FF // FF_CHUNK
HALO = BF16_SUBLANES


def _rms(x, gain):
    ms = jnp.mean(x * x, axis=-1, keepdims=True)
    return x * lax.rsqrt(ms + EPS) * gain


def _resident(shape):
    zeros = (0,) * len(shape)
    return pl.BlockSpec(shape, lambda *_: zeros, pipeline_mode=pl.Buffered(1))


def _params(n_axes):
    return pltpu.CompilerParams(
        dimension_semantics=("arbitrary",) * n_axes,
        vmem_limit_bytes=VMEM_LIMIT_BYTES,
    )


def _in_proj_kernel(x_ref, g_ref, w_ref, qg_ref, kg_ref, cos_ref, sin_ref, hm_ref,
                    qa_ref, ka_ref, va_ref, qb_ref, kb_ref, vb_ref):
    n = _rms(x_ref[...], g_ref[...]).astype(BF16)
    proj = jnp.dot(n, w_ref[...], preferred_element_type=F32)
    cos = cos_ref[...]
    sin = sin_ref[...]
    lane = lax.broadcasted_iota(jnp.int32, (1, LANES), 1)
    first_half = (lane & ROPE_HALF) == 0
    head_mean = hm_ref[...]

    def head_norm(t, gain):
        w = t.shape[1]
        ms = jnp.dot((t * t).astype(BF16), head_mean[:w, :w], preferred_element_type=F32)
        return t * lax.rsqrt(ms + EPS) * gain

    def rope(t):
        ahead = pltpu.roll(t, LANES - ROPE_HALF, axis=1)
        behind = pltpu.roll(t, ROPE_HALF, axis=1)
        return t * cos + jnp.where(first_half, ahead, behind) * sin

    qa = head_norm(proj[:, :WIDTH_A], qg_ref[...])
    for j in range(N_TILES):
        qa_ref[j] = (rope(qa[:, j * LANES:(j + 1) * LANES]) * Q_SCALE).astype(BF16)
    off = WIDTH_A
    ka_ref[...] = rope(head_norm(proj[:, off:off + KV_WIDTH_A], kg_ref[...])).astype(BF16)
    off += KV_WIDTH_A
    va_ref[...] = proj[:, off:off + KV_WIDTH_A].astype(BF16)
    off += KV_WIDTH_A
    for j in range(N_TILES):
        lo = off + j * LANES
        qb_ref[j] = (proj[:, lo:lo + LANES] * Q_SCALE).astype(BF16)
        kb_ref[j] = proj[:, lo + WIDTH_B:lo + WIDTH_B + LANES].astype(BF16)
        vb_ref[j] = proj[:, lo + 2 * WIDTH_B:lo + 2 * WIDTH_B + LANES].astype(BF16)


def _in_proj(x2, attn_norm, w_in, q_gain, k_gain, cos_t, sin_t, head_mean, seq):
    t = x2.shape[0]
    tm = TM_PROJ
    assert t % tm == 0 and seq % tm == 0
    tiles_per_seq = seq // tm
    row = lambda i: (i, 0)
    tile3 = lambda i: (0, i, 0)
    pos = lambda i: (i % tiles_per_seq, 0)
    tiled = jax.ShapeDtypeStruct((N_TILES, t, LANES), BF16)
    flat = jax.ShapeDtypeStruct((t, KV_WIDTH_A), BF16)
    return pl.pallas_call(
        _in_proj_kernel,
        grid=(t // tm,),
        in_specs=[
            pl.BlockSpec((tm, D_MODEL), row),
            _resident((1, D_MODEL)),
            _resident((D_MODEL, IN_PROJ_WIDTH)),
            _resident((1, WIDTH_A)),
            _resident((1, KV_WIDTH_A)),
            pl.BlockSpec((tm, LANES), pos),
            pl.BlockSpec((tm, LANES), pos),
            _resident((WIDTH_A, WIDTH_A)),
        ],
        out_specs=[
            pl.BlockSpec((N_TILES, tm, LANES), tile3),
            pl.BlockSpec((tm, KV_WIDTH_A), row),
            pl.BlockSpec((tm, KV_WIDTH_A), row),
            pl.BlockSpec((N_TILES, tm, LANES), tile3),
            pl.BlockSpec((N_TILES, tm, LANES), tile3),
            pl.BlockSpec((N_TILES, tm, LANES), tile3),
        ],
        out_shape=[tiled, flat, flat, tiled, tiled, tiled],
        compiler_params=_params(1),
        name="in_proj",
    )(x2, attn_norm, w_in, q_gain, k_gain, cos_t, sin_t, head_mean)


def _attention_pipeline(heads, score_fn, value_fn, s_refs, p_refs, acc_ref, low):
    n = len(heads)
    denoms = {}
    written = set()
    for t in range(n + 2):
        if t >= 2:
            h = t - 2
            tile, half = heads[h]
            o = jnp.dot(p_refs[h % 2][...], value_fn(tile), preferred_element_type=F32) / denoms.pop(h)
            if tile in written:
                prev = acc_ref[tile]
                o = jnp.where(low, prev, o) if half else jnp.where(low, o, prev)
            acc_ref[tile] = o
            written.add(tile)
        if 1 <= t <= n:
            h = t - 1
            s = s_refs[h % 2][...]
            p = jnp.exp2(s - jnp.max(s, axis=-1, keepdims=True))
            denoms[h] = jnp.sum(p, axis=-1, keepdims=True)
            p_refs[h % 2][...] = p.astype(BF16)
        if t < n:
            s_refs[t % 2][...] = score_fn(*heads[t])


_HEAD_ORDER = [(tile, half) for tile in range(N_TILES) for half in range(2)]


def _group_norm_store(acc_ref, gain_ref, o_ref):
    tiles = [acc_ref[j] for j in range(N_TILES)]
    ss = sum(jnp.sum(o * o, axis=-1, keepdims=True) for o in tiles)
    inv = lax.rsqrt(ss * (1.0 / (N_TILES * LANES)) + EPS)
    for j in range(N_TILES):
        sl = slice(j * LANES, (j + 1) * LANES)
        o_ref[:, sl] = (tiles[j] * inv * gain_ref[:, sl]).astype(BF16)


_NT = (((1,), (1,)), ((), ()))


def _gqa_kernel(q_ref, k_ref, v_ref, gain_ref, o_ref, acc_ref, s0_ref, s1_ref, p0_ref, p1_ref):
    low = lax.broadcasted_iota(jnp.int32, (1, LANES), 1) < HEAD_DIM

    def scores(tile, half):
        qt = q_ref[tile]
        qm = jnp.where(jnp.logical_not(low) if half else low, qt, jnp.zeros_like(qt))
        return lax.dot_general(qm, k_ref[...], _NT, preferred_element_type=F32)

    _attention_pipeline(_HEAD_ORDER, scores, lambda tile: v_ref[...],
                        (s0_ref, s1_ref), (p0_ref, p1_ref), acc_ref, low)
    _group_norm_store(acc_ref, gain_ref, o_ref)


def _gqa(qa, ka, va, gain, batch, seq):
    t = batch * seq
    tq = TQ_GQA
    assert seq % tq == 0
    nq = seq // tq
    return pl.pallas_call(
        _gqa_kernel,
        grid=(batch, nq),
        in_specs=[
            pl.BlockSpec((N_TILES, tq, LANES), lambda b, i: (0, b * nq + i, 0)),
            pl.BlockSpec((seq, KV_WIDTH_A), lambda b, i: (b, 0)),
            pl.BlockSpec((seq, KV_WIDTH_A), lambda b, i: (b, 0)),
            _resident((1, WIDTH_A)),
        ],
        out_specs=pl.BlockSpec((tq, WIDTH_A), lambda b, i: (b * nq + i, 0)),
        out_shape=jax.ShapeDtypeStruct((t, WIDTH_A), BF16),
        scratch_shapes=[pltpu.VMEM((N_TILES, tq, LANES), F32),
                        pltpu.VMEM((tq, seq), F32), pltpu.VMEM((tq, seq), F32),
                        pltpu.VMEM((tq, seq), BF16), pltpu.VMEM((tq, seq), BF16)],
        compiler_params=_params(2),
        name="gqa",
    )(qa, ka, va, gain)


def _na_kernel(q_ref, k_ref, v_ref, bias_ref, gain_ref, o_ref, acc_ref, s0_ref, s1_ref, p0_ref, p1_ref,
               *, rows):
    g = pl.program_id(1)
    n_groups = rows // NA_QROWS
    w0 = jnp.clip(NA_QROWS * g - NA_KH // 2, 0, rows - NA_WROWS)
    start = pl.multiple_of(w0 * GRID_W, GRID_W)
    kind = jnp.where(g == 0, 0, jnp.where(g == n_groups - 1, 2, 1))
    low = lax.broadcasted_iota(jnp.int32, (1, LANES), 1) < HEAD_DIM
    window = pl.ds(start, NA_WROWS * GRID_W)

    def scores(tile, half):
        qt = q_ref[tile]
        qm = jnp.where(jnp.logical_not(low) if half else low, qt, jnp.zeros_like(qt))
        s = lax.dot_general(qm, k_ref[tile, window, :], _NT, preferred_element_type=F32)
        return s + bias_ref[kind, 2 * tile + half]

    _attention_pipeline(_HEAD_ORDER, scores, lambda tile: v_ref[tile, window, :],
                        (s0_ref, s1_ref), (p0_ref, p1_ref), acc_ref, low)
    _group_norm_store(acc_ref, gain_ref, o_ref)


def _na_bias_table(rpb):
    c = np.arange(GRID_W)
    c_start = np.clip(c - NA_KW // 2, 0, GRID_W - NA_KW)
    col_ok = (c[None, :] >= c_start[:, None]) & (c[None, :] < c_start[:, None] + NA_KW)
    rl = np.arange(NA_QROWS)[:, None]
    kl = np.arange(NA_WROWS)[None, :]
    half = NA_KH // 2
    assert NA_QROWS <= half
    first = (kl < NA_KH, kl - rl + NA_KH - 1)
    interior = ((kl - rl >= 0) & (kl - rl < NA_KH), kl - rl + NA_KH - 1 - half)
    last = (kl >= NA_WROWS - NA_KH, kl - rl - NA_WROWS + NA_QROWS + NA_KH - 1)
    row_ok = np.stack([np.broadcast_to(k[0], (NA_QROWS, NA_WROWS)) for k in (first, interior, last)])
    dr_idx = np.stack([np.clip(k[1], 0, 2 * NA_KH - 2) for k in (first, interior, last)])
    period = 2 * GRID_W - 1
    pad_lo = GRID_W - NA_KW
    base = jnp.pad(rpb.astype(F32), ((0, 0), (0, 0), (pad_lo, period - pad_lo - (2 * NA_KW - 1))))
    skew = jnp.tile(base, (1, 1, GRID_W + 1))[:, :, :GRID_W * 2 * GRID_W]
    skew = skew.reshape(N_HEADS_B, 2 * NA_KH - 1, GRID_W, 2 * GRID_W)[:, :, :, :GRID_W]
    rpb_c = jnp.flip(skew, axis=2)
    rpb_c = jnp.where(col_ok[None, None], rpb_c * LOG2E, MASK_VALUE)
    masked = jnp.full((N_HEADS_B, GRID_W, GRID_W), MASK_VALUE, F32)
    kinds = []
    for kind in range(3):
        q_rows = []
        for r in range(NA_QROWS):
            blocks = [rpb_c[:, int(dr_idx[kind, r, k])] if row_ok[kind, r, k] else masked
                      for k in range(NA_WROWS)]
            q_rows.append(jnp.concatenate(blocks, axis=2))
        kinds.append(jnp.concatenate(q_rows, axis=1))
    return jnp.stack(kinds, axis=0)


def _natten(qb, kb, vb, bias, gain, batch, seq):
    t = batch * seq
    rows = seq // GRID_W
    assert seq % GRID_W == 0 and rows % NA_QROWS == 0 and rows >= NA_WROWS
    n_groups = rows // NA_QROWS
    tq = NA_QROWS * GRID_W
    return pl.pallas_call(
        functools.partial(_na_kernel, rows=rows),
        grid=(batch, n_groups),
        in_specs=[
            pl.BlockSpec((N_TILES, tq, LANES), lambda b, g: (0, b * n_groups + g, 0)),
            pl.BlockSpec((N_TILES, seq, LANES), lambda b, g: (0, b, 0)),
            pl.BlockSpec((N_TILES, seq, LANES), lambda b, g: (0, b, 0)),
            _resident(bias.shape),
            _resident((1, WIDTH_B)),
        ],
        out_specs=pl.BlockSpec((tq, WIDTH_B), lambda b, g: (b * n_groups + g, 0)),
        out_shape=jax.ShapeDtypeStruct((t, WIDTH_B), BF16),
        scratch_shapes=[pltpu.VMEM((N_TILES, tq, LANES), F32),
                        pltpu.VMEM((tq, NA_WROWS * GRID_W), F32), pltpu.VMEM((tq, NA_WROWS * GRID_W), F32),
                        pltpu.VMEM((tq, NA_WROWS * GRID_W), BF16), pltpu.VMEM((tq, NA_WROWS * GRID_W), BF16)],
        compiler_params=_params(2),
        name="natten",
    )(qb, kb, vb, bias, gain)


def _out_proj_kernel(ma_ref, mb_ref, x_ref, w_ref, g_ref, h_ref, n_ref):
    mixed = jnp.concatenate([ma_ref[...], mb_ref[...]], axis=1)
    h = x_ref[...] + jnp.dot(mixed, w_ref[...], preferred_element_type=F32)
    h_ref[...] = h
    n_ref[...] = _rms(h, g_ref[...]).astype(BF16)


def _out_proj(mixed_a, mixed_b, x2, w_out, ffn_norm):
    t = x2.shape[0]
    tm = TM_PROJ
    row = lambda i: (i, 0)
    return pl.pallas_call(
        _out_proj_kernel,
        grid=(t // tm,),
        in_specs=[
            pl.BlockSpec((tm, WIDTH_A), row),
            pl.BlockSpec((tm, WIDTH_B), row),
            pl.BlockSpec((tm, D_MODEL), row),
            _resident((WIDTH_A + WIDTH_B, D_MODEL)),
            _resident((1, D_MODEL)),
        ],
        out_specs=[pl.BlockSpec((tm, D_MODEL), row), pl.BlockSpec((tm, D_MODEL), row)],
        out_shape=[jax.ShapeDtypeStruct((t, D_MODEL), F32), jax.ShapeDtypeStruct((t, D_MODEL), BF16)],
        compiler_params=_params(1),
        name="out_proj",
    )(mixed_a, mixed_b, x2, w_out, ffn_norm)


def _gelu_tanh(x):
    k = -2.0 * math.sqrt(2.0 / math.pi) * math.log2(math.e)
    return x / (1.0 + jnp.exp2(x * (k + (k * 0.044715) * (x * x))))


def _ffn_kernel(n_ref, prev_ref, next_ref, h_ref, p_ref, wup_ref, cw_ref, cb_ref, wdn_ref,
                pn_ref, wg_ref, wp_ref, fn_ref, y_ref, ext_ref, u_ref, act_ref, acc_ref,
                *, tiles_per_seq):
    i = pl.program_id(0)
    tm = n_ref.shape[0]
    pos = i % tiles_per_seq
    prev = prev_ref[...]
    nxt = next_ref[...]
    ext_ref[0:HALO] = jnp.where(pos != 0, prev, jnp.zeros_like(prev))
    ext_ref[HALO:HALO + tm] = n_ref[...]
    ext_ref[HALO + tm:] = jnp.where(pos != tiles_per_seq - 1, nxt, jnp.zeros_like(nxt))

    def chunk_cols(ref, c):
        lo = c * FF_CHUNK
        return jnp.concatenate([ref[:, lo:lo + FF_CHUNK], ref[:, D_FF + lo:D_FF + lo + FF_CHUNK]], axis=1)

    for t in range(N_FF_CHUNKS + 2):
        if t < N_FF_CHUNKS:
            u_ref[t % 2] = jnp.dot(ext_ref[...], chunk_cols(wup_ref, t), preferred_element_type=F32)
        if 1 <= t <= N_FF_CHUNKS:
            c = t - 1
            cw = chunk_cols(cw_ref, c)
            conv = (u_ref[c % 2, HALO - 1:HALO - 1 + tm] * cw[0:1]
                    + u_ref[c % 2, HALO:HALO + tm] * cw[1:2]
                    + u_ref[c % 2, HALO + 1:HALO + 1 + tm] * cw[2:3]
                    + chunk_cols(cb_ref, c))
            act = conv[:, :FF_CHUNK] * _gelu_tanh(conv[:, FF_CHUNK:])
            act_ref[c % 2] = act.astype(BF16)
        if t >= 2:
            c = t - 2
            d = jnp.dot(act_ref[c % 2], wdn_ref[c * FF_CHUNK:(c + 1) * FF_CHUNK, :],
                        preferred_element_type=F32)
            if c == 0:
                acc_ref[...] = d
            else:
                acc_ref[...] += d


    h = h_ref[...] + acc_ref[...]
    gate = jax.nn.sigmoid(jnp.dot(_rms(h, pn_ref[...]).astype(BF16), wg_ref[...],
                                  preferred_element_type=F32))
    emb = jnp.dot(p_ref[...].astype(BF16), wp_ref[...], preferred_element_type=F32)
    h = h + gate * emb
    y_ref[...] = _rms(h, fn_ref[...])


def _ffn(n2, h1, p2, w_up, conv_w, conv_b, w_down, ple_norm, w_gate, w_ple, final_norm, seq):
    t = n2.shape[0]
    tm = TM_FFN
    assert t % tm == 0 and seq % tm == 0 and tm % HALO == 0
    tiles_per_seq = seq // tm
    halo_per_tile = tm // HALO
    n_halo_blocks = t // HALO
    row = lambda i: (i, 0)
    return pl.pallas_call(
        functools.partial(_ffn_kernel, tiles_per_seq=tiles_per_seq),
        grid=(t // tm,),
        in_specs=[
            pl.BlockSpec((tm, D_MODEL), row),
            pl.BlockSpec((HALO, D_MODEL), lambda i: (jnp.maximum(i * halo_per_tile - 1, 0), 0)),
            pl.BlockSpec((HALO, D_MODEL),
                         lambda i: (jnp.minimum((i + 1) * halo_per_tile, n_halo_blocks - 1), 0)),
            pl.BlockSpec((tm, D_MODEL), row),
            pl.BlockSpec((tm, PLE_DIM), row),
            _resident((D_MODEL, 2 * D_FF)),
            _resident((3, 2 * D_FF)),
            _resident((1, 2 * D_FF)),
            _resident((D_FF, D_MODEL)),
            _resident((1, D_MODEL)),
            _resident((D_MODEL, D_MODEL)),
            _resident((PLE_DIM, D_MODEL)),
            _resident((1, D_MODEL)),
        ],
        out_specs=pl.BlockSpec((tm, D_MODEL), row),
        out_shape=jax.ShapeDtypeStruct((t, D_MODEL), F32),
        scratch_shapes=[pltpu.VMEM((tm + 2 * HALO, D_MODEL), BF16),
                        pltpu.VMEM((2, tm + 2 * HALO, 2 * FF_CHUNK), F32),
                        pltpu.VMEM((2, tm, FF_CHUNK), BF16),
                        pltpu.VMEM((tm, D_MODEL), F32)],
        compiler_params=_params(1),
        name="ffn",
    )(n2, n2, n2, h1, p2, w_up, conv_w, conv_b, w_down, ple_norm, w_gate, w_ple, final_norm)


def _pair_heads(w, axis):
    group = N_Q_A // N_KV_A
    shape = w.shape
    split = shape[:axis] + (N_KV_A, group, HEAD_DIM) + shape[axis + 1:]
    return jnp.swapaxes(w.reshape(split), axis, axis + 1).reshape(shape)


def _rope_tables(seq):
    rows = seq // GRID_W
    freqs = ROPE_THETA ** (-jnp.arange(0, ROPE_AXIS_DIM, 2, dtype=F32) / ROPE_AXIS_DIM)
    ang = jnp.arange(max(rows, GRID_W), dtype=F32)[:, None] * freqs[None]
    cos_i, sin_i = jnp.cos(ang), jnp.sin(ang)

    def expand(first, second):
        pair = jnp.concatenate([first, second], axis=1)
        by_row = jnp.broadcast_to(pair[:rows, None, :], (rows, GRID_W, ROPE_AXIS_DIM))
        by_col = jnp.broadcast_to(pair[None, :GRID_W, :], (rows, GRID_W, ROPE_AXIS_DIM))
        return jnp.concatenate([by_row, by_col], axis=2).reshape(seq, HEAD_DIM)

    cos_h = expand(cos_i, cos_i)
    sin_h = expand(-sin_i, sin_i)
    return jnp.concatenate([cos_h, cos_h], axis=1), jnp.concatenate([sin_h, sin_h], axis=1)


def _prepare_params(attn_norm, w_in, q_norm_a, k_norm_a, rpb_b, out_norm_a, out_norm_b, w_out,
                    ffn_norm, w_up, conv_w, conv_b, w_down, ple_norm, w_ple_gate, w_ple, final_norm):
    head_id = np.arange(WIDTH_A) // HEAD_DIM
    head_mean = jnp.asarray((head_id[:, None] == head_id[None, :]) / HEAD_DIM, BF16)
    row2 = lambda v: v.reshape(1, -1).astype(F32)
    w_in_b = w_in[0].astype(BF16)
    w_out_b = w_out[0].astype(BF16)
    return dict(
        attn_norm=row2(attn_norm[0]),
        w_in=jnp.concatenate([_pair_heads(w_in_b[:, :WIDTH_A], 1), w_in_b[:, WIDTH_A:]], axis=1),
        q_gain=row2(jnp.tile(q_norm_a[0], N_Q_A)),
        k_gain=row2(jnp.tile(k_norm_a[0], N_KV_A)),
        head_mean=head_mean,
        na_bias=_na_bias_table(rpb_b[0]),
        gain_a=row2(_pair_heads(out_norm_a[0], 0)),
        gain_b=row2(out_norm_b[0]),
        w_out=jnp.concatenate([_pair_heads(w_out_b[:WIDTH_A], 0), w_out_b[WIDTH_A:]], axis=0),
        ffn_norm=row2(ffn_norm[0]),
        w_up=w_up[0].astype(BF16),
        conv_w=conv_w[0].astype(F32),
        conv_b=row2(conv_b[0]),
        w_down=w_down[0].astype(BF16),
        ple_norm=row2(ple_norm[0]),
        w_gate=w_ple_gate[0].astype(BF16),
        w_ple=w_ple[0].astype(BF16),
        final_norm=row2(final_norm),
    )


def _trunk(x, p, w, rope):
    batch, seq, _ = x.shape
    t = batch * seq
    x2 = x.reshape(t, D_MODEL)
    p2 = p[0].reshape(t, PLE_DIM)
    cos_t, sin_t = rope
    assert cos_t.shape[0] >= seq
    qa, ka, va, qb, kb, vb = _in_proj(x2, w["attn_norm"], w["w_in"], w["q_gain"], w["k_gain"],
                                      cos_t, sin_t, w["head_mean"], seq)
    mixed_a = _gqa(qa, ka, va, w["gain_a"], batch, seq)
    mixed_b = _natten(qb, kb, vb, w["na_bias"], w["gain_b"], batch, seq)
    h1, n2 = _out_proj(mixed_a, mixed_b, x2, w["w_out"], w["ffn_norm"])
    y = _ffn(n2, h1, p2, w["w_up"], w["conv_w"], w["conv_b"], w["w_down"], w["ple_norm"],
             w["w_gate"], w["w_ple"], w["final_norm"], seq)
    return y.reshape(batch, seq, D_MODEL)


def kernel(x_prompt, x_sample, p_prompt, p_sample, attn_norm, w_in, q_norm_a, k_norm_a, rpb_b,
           out_norm_a, out_norm_b, w_out, ffn_norm, w_up, conv_w, conv_b, w_down, ple_norm,
           w_ple_gate, w_ple, final_norm):
    w = _prepare_params(attn_norm, w_in, q_norm_a, k_norm_a, rpb_b, out_norm_a, out_norm_b, w_out,
                        ffn_norm, w_up, conv_w, conv_b, w_down, ple_norm, w_ple_gate, w_ple, final_norm)
    rope = _rope_tables(max(x_prompt.shape[1], x_sample.shape[1]))
    return (_trunk(x_prompt, p_prompt, w, rope), _trunk(x_sample, p_sample, w, rope))
```

```python
import functools
import math

import numpy as np
import jax
import jax.numpy as jnp
from jax import lax
from jax.experimental import pallas as pl
from jax.experimental.pallas import tpu as pltpu

F32 = jnp.float32
BF16 = jnp.bfloat16

D_MODEL = 1024
HEAD_DIM = 64
N_Q_A = 8
N_KV_A = 2
N_HEADS_B = 8
WIDTH_A = N_Q_A * HEAD_DIM
WIDTH_B = N_HEADS_B * HEAD_DIM
KV_WIDTH_A = N_KV_A * HEAD_DIM
IN_PROJ_WIDTH = WIDTH_A + 2 * KV_WIDTH_A + 3 * WIDTH_B
GRID_W = 64
ROPE_AXIS_DIM = HEAD_DIM // 2
ROPE_HALF = ROPE_AXIS_DIM // 2
ROPE_THETA = 10000.0
NA_KH = 8
NA_KW = 16
D_FF = 2816
PLE_DIM = 256
EPS = 1e-6
LOG2E = math.log2(math.e)
Q_SCALE = LOG2E / math.sqrt(HEAD_DIM)

LANES = 128
BF16_SUBLANES = 16
VMEM_LIMIT_BYTES = 56 * 1024 * 1024

N_TILES = WIDTH_A // LANES
MASK_VALUE = -1e30

TM_PROJ = 1024
TQ_GQA = 512
NA_QROWS = 4
NA_WROWS = NA_QROWS + NA_KH
TM_FFN = 512
FF_CHUNK = 256
N_FF_CHUNKS = D_FF // FF_CHUNK
DOWN_GROUP = 11
ACT_SLOTS = DOWN_GROUP + 2
HALO = BF16_SUBLANES


def _rms(x, gain):
    ms = jnp.mean(x * x, axis=-1, keepdims=True)
    return x * lax.rsqrt(ms + EPS) * gain


def _resident(shape):
    zeros = (0,) * len(shape)
    return pl.BlockSpec(shape, lambda *_: zeros, pipeline_mode=pl.Buffered(1))


def _params(n_axes):
    return pltpu.CompilerParams(
        dimension_semantics=("arbitrary",) * n_axes,
        vmem_limit_bytes=VMEM_LIMIT_BYTES,
    )


def _in_proj_kernel(x_ref, g_ref, w_ref, qg_ref, kg_ref, cos_ref, sin_ref, hm_ref,
                    qa_ref, ka_ref, va_ref, qb_ref, kb_ref, vb_ref):
    n = _rms(x_ref[...], g_ref[...]).astype(BF16)
    proj = jnp.dot(n, w_ref[...], preferred_element_type=F32)
    cos = cos_ref[...]
    sin = sin_ref[...]
    lane = lax.broadcasted_iota(jnp.int32, (1, LANES), 1)
    first_half = (lane & ROPE_HALF) == 0
    head_mean = hm_ref[...]

    def head_norm(t, gain):
        w = t.shape[1]
        ms = jnp.dot((t * t).astype(BF16), head_mean[:w, :w], preferred_element_type=F32)
        return t * lax.rsqrt(ms + EPS) * gain

    def rope(t):
        ahead = pltpu.roll(t, LANES - ROPE_HALF, axis=1)
        behind = pltpu.roll(t, ROPE_HALF, axis=1)
        return t * cos + jnp.where(first_half, ahead, behind) * sin

    qa = head_norm(proj[:, :WIDTH_A], qg_ref[...])
    for j in range(N_TILES):
        qa_ref[j] = (rope(qa[:, j * LANES:(j + 1) * LANES]) * Q_SCALE).astype(BF16)
    off = WIDTH_A
    ka_ref[...] = rope(head_norm(proj[:, off:off + KV_WIDTH_A], kg_ref[...])).astype(BF16)
    off += KV_WIDTH_A
    va_ref[...] = proj[:, off:off + KV_WIDTH_A].astype(BF16)
    off += KV_WIDTH_A
    for j in range(N_TILES):
        lo = off + j * LANES
        qb_ref[j] = (proj[:, lo:lo + LANES] * Q_SCALE).astype(BF16)
        kb_ref[j] = proj[:, lo + WIDTH_B:lo + WIDTH_B + LANES].astype(BF16)
        vb_ref[j] = proj[:, lo + 2 * WIDTH_B:lo + 2 * WIDTH_B + LANES].astype(BF16)


def _in_proj(x2, attn_norm, w_in, q_gain, k_gain, cos_t, sin_t, head_mean, seq):
    t = x2.shape[0]
    tm = TM_PROJ
    assert t % tm == 0 and seq % tm == 0
    tiles_per_seq = seq // tm
    row = lambda i: (i, 0)
    tile3 = lambda i: (0, i, 0)
    pos = lambda i: (i % tiles_per_seq, 0)
    tiled = jax.ShapeDtypeStruct((N_TILES, t, LANES), BF16)
    flat = jax.ShapeDtypeStruct((t, KV_WIDTH_A), BF16)
    return pl.pallas_call(
        _in_proj_kernel,
        grid=(t // tm,),
        in_specs=[
            pl.BlockSpec((tm, D_MODEL), row),
            _resident((1, D_MODEL)),
            _resident((D_MODEL, IN_PROJ_WIDTH)),
            _resident((1, WIDTH_A)),
            _resident((1, KV_WIDTH_A)),
            pl.BlockSpec((tm, LANES), pos),
            pl.BlockSpec((tm, LANES), pos),
            _resident((WIDTH_A, WIDTH_A)),
        ],
        out_specs=[
            pl.BlockSpec((N_TILES, tm, LANES), tile3),
            pl.BlockSpec((tm, KV_WIDTH_A), row),
            pl.BlockSpec((tm, KV_WIDTH_A), row),
            pl.BlockSpec((N_TILES, tm, LANES), tile3),
            pl.BlockSpec((N_TILES, tm, LANES), tile3),
            pl.BlockSpec((N_TILES, tm, LANES), tile3),
        ],
        out_shape=[tiled, flat, flat, tiled, tiled, tiled],
        compiler_params=_params(1),
        name="in_proj",
    )(x2, attn_norm, w_in, q_gain, k_gain, cos_t, sin_t, head_mean)


def _attention_pipeline(heads, score_fn, value_fn, s_refs, p_refs, acc_ref, low):
    n = len(heads)
    denoms = {}
    written = set()
    for t in range(n + 2):
        if t >= 2:
            h = t - 2
            tile, half = heads[h]
            o = jnp.dot(p_refs[h % 2][...], value_fn(tile), preferred_element_type=F32) / denoms.pop(h)
            if tile in written:
                prev = acc_ref[tile]
                o = jnp.where(low, prev, o) if half else jnp.where(low, o, prev)
            acc_ref[tile] = o
            written.add(tile)
        if 1 <= t <= n:
            h = t - 1
            s = s_refs[h % 2][...]
            p = jnp.exp2(s - jnp.max(s, axis=-1, keepdims=True))
            denoms[h] = jnp.sum(p, axis=-1, keepdims=True)
            p_refs[h % 2][...] = p.astype(BF16)
        if t < n:
            s_refs[t % 2][...] = score_fn(*heads[t])


_HEAD_ORDER = [(tile, half) for tile in range(N_TILES) for half in range(2)]


def _group_norm_store(acc_ref, gain_ref, o_ref):
    tiles = [acc_ref[j] for j in range(N_TILES)]
    ss = sum(jnp.sum(o * o, axis=-1, keepdims=True) for o in tiles)
    inv = lax.rsqrt(ss * (1.0 / (N_TILES * LANES)) + EPS)
    for j in range(N_TILES):
        sl = slice(j * LANES, (j + 1) * LANES)
        o_ref[:, sl] = (tiles[j] * inv * gain_ref[:, sl]).astype(BF16)


_NT = (((1,), (1,)), ((), ()))


def _gqa_kernel(q_ref, k_ref, v_ref, gain_ref, o_ref, acc_ref, s0_ref, s1_ref, p0_ref, p1_ref):
    low = lax.broadcasted_iota(jnp.int32, (1, LANES), 1) < HEAD_DIM

    def scores(tile, half):
        qt = q_ref[tile]
        qm = jnp.where(jnp.logical_not(low) if half else low, qt, jnp.zeros_like(qt))
        return lax.dot_general(qm, k_ref[...], _NT, preferred_element_type=F32)

    _attention_pipeline(_HEAD_ORDER, scores, lambda tile: v_ref[...],
                        (s0_ref, s1_ref), (p0_ref, p1_ref), acc_ref, low)
    _group_norm_store(acc_ref, gain_ref, o_ref)


def _gqa(qa, ka, va, gain, batch, seq):
    t = batch * seq
    tq = TQ_GQA
    assert seq % tq == 0
    nq = seq // tq
    return pl.pallas_call(
        _gqa_kernel,
        grid=(batch, nq),
        in_specs=[
            pl.BlockSpec((N_TILES, tq, LANES), lambda b, i: (0, b * nq + i, 0)),
            pl.BlockSpec((seq, KV_WIDTH_A), lambda b, i: (b, 0)),
            pl.BlockSpec((seq, KV_WIDTH_A), lambda b, i: (b, 0)),
            _resident((1, WIDTH_A)),
        ],
        out_specs=pl.BlockSpec((tq, WIDTH_A), lambda b, i: (b * nq + i, 0)),
        out_shape=jax.ShapeDtypeStruct((t, WIDTH_A), BF16),
        scratch_shapes=[pltpu.VMEM((N_TILES, tq, LANES), F32),
                        pltpu.VMEM((tq, seq), F32), pltpu.VMEM((tq, seq), F32),
                        pltpu.VMEM((tq, seq), BF16), pltpu.VMEM((tq, seq), BF16)],
        compiler_params=_params(2),
        name="gqa",
    )(qa, ka, va, gain)


def _na_kernel(q_ref, k_ref, v_ref, bias_ref, gain_ref, o_ref, acc_ref, s0_ref, s1_ref, p0_ref, p1_ref,
               *, rows):
    g = pl.program_id(1)
    n_groups = rows // NA_QROWS
    w0 = jnp.clip(NA_QROWS * g - NA_KH // 2, 0, rows - NA_WROWS)
    start = pl.multiple_of(w0 * GRID_W, GRID_W)
    kind = jnp.where(g == 0, 0, jnp.where(g == n_groups - 1, 2, 1))
    low = lax.broadcasted_iota(jnp.int32, (1, LANES), 1) < HEAD_DIM
    window = pl.ds(start, NA_WROWS * GRID_W)

    def scores(tile, half):
        qt = q_ref[tile]
        qm = jnp.where(jnp.logical_not(low) if half else low, qt, jnp.zeros_like(qt))
        s = lax.dot_general(qm, k_ref[tile, window, :], _NT, preferred_element_type=F32)
        return s + bias_ref[kind, 2 * tile + half]

    _attention_pipeline(_HEAD_ORDER, scores, lambda tile: v_ref[tile, window, :],
                        (s0_ref, s1_ref), (p0_ref, p1_ref), acc_ref, low)
    _group_norm_store(acc_ref, gain_ref, o_ref)


def _na_bias_table(rpb):
    c = np.arange(GRID_W)
    c_start = np.clip(c - NA_KW // 2, 0, GRID_W - NA_KW)
    col_ok = (c[None, :] >= c_start[:, None]) & (c[None, :] < c_start[:, None] + NA_KW)
    rl = np.arange(NA_QROWS)[:, None]
    kl = np.arange(NA_WROWS)[None, :]
    half = NA_KH // 2
    assert NA_QROWS <= half
    first = (kl < NA_KH, kl - rl + NA_KH - 1)
    interior = ((kl - rl >= 0) & (kl - rl < NA_KH), kl - rl + NA_KH - 1 - half)
    last = (kl >= NA_WROWS - NA_KH, kl - rl - NA_WROWS + NA_QROWS + NA_KH - 1)
    row_ok = np.stack([np.broadcast_to(k[0], (NA_QROWS, NA_WROWS)) for k in (first, interior, last)])
    dr_idx = np.stack([np.clip(k[1], 0, 2 * NA_KH - 2) for k in (first, interior, last)])
    period = 2 * GRID_W - 1
    pad_lo = GRID_W - NA_KW
    base = jnp.pad(rpb.astype(F32), ((0, 0), (0, 0), (pad_lo, period - pad_lo - (2 * NA_KW - 1))))
    skew = jnp.tile(base, (1, 1, GRID_W + 1))[:, :, :GRID_W * 2 * GRID_W]
    skew = skew.reshape(N_HEADS_B, 2 * NA_KH - 1, GRID_W, 2 * GRID_W)[:, :, :, :GRID_W]
    rpb_c = jnp.flip(skew, axis=2)
    rpb_c = jnp.where(col_ok[None, None], rpb_c * LOG2E, MASK_VALUE)
    masked = jnp.full((N_HEADS_B, GRID_W, GRID_W), MASK_VALUE, F32)
    kinds = []
    for kind in range(3):
        q_rows = []
        for r in range(NA_QROWS):
            blocks = [rpb_c[:, int(dr_idx[kind, r, k])] if row_ok[kind, r, k] else masked
                      for k in range(NA_WROWS)]
            q_rows.append(jnp.concatenate(blocks, axis=2))
        kinds.append(jnp.concatenate(q_rows, axis=1))
    return jnp.stack(kinds, axis=0)


def _natten(qb, kb, vb, bias, gain, batch, seq):
    t = batch * seq
    rows = seq // GRID_W
    assert seq % GRID_W == 0 and rows % NA_QROWS == 0 and rows >= NA_WROWS
    n_groups = rows // NA_QROWS
    tq = NA_QROWS * GRID_W
    return pl.pallas_call(
        functools.partial(_na_kernel, rows=rows),
        grid=(batch, n_groups),
        in_specs=[
            pl.BlockSpec((N_TILES, tq, LANES), lambda b, g: (0, b * n_groups + g, 0)),
            pl.BlockSpec((N_TILES, seq, LANES), lambda b, g: (0, b, 0)),
            pl.BlockSpec((N_TILES, seq, LANES), lambda b, g: (0, b, 0)),
            _resident(bias.shape),
            _resident((1, WIDTH_B)),
        ],
        out_specs=pl.BlockSpec((tq, WIDTH_B), lambda b, g: (b * n_groups + g, 0)),
        out_shape=jax.ShapeDtypeStruct((t, WIDTH_B), BF16),
        scratch_shapes=[pltpu.VMEM((N_TILES, tq, LANES), F32),
                        pltpu.VMEM((tq, NA_WROWS * GRID_W), F32), pltpu.VMEM((tq, NA_WROWS * GRID_W), F32),
                        pltpu.VMEM((tq, NA_WROWS * GRID_W), BF16), pltpu.VMEM((tq, NA_WROWS * GRID_W), BF16)],
        compiler_params=_params(2),
        name="natten",
    )(qb, kb, vb, bias, gain)


def _out_proj_kernel(ma_ref, mb_ref, x_ref, w_ref, g_ref, h_ref, n_ref):
    mixed = jnp.concatenate([ma_ref[...], mb_ref[...]], axis=1)
    h = x_ref[...] + jnp.dot(mixed, w_ref[...], preferred_element_type=F32)
    h_ref[...] = h
    n_ref[...] = _rms(h, g_ref[...]).astype(BF16)


def _out_proj(mixed_a, mixed_b, x2, w_out, ffn_norm):
    t = x2.shape[0]
    tm = TM_PROJ
    row = lambda i: (i, 0)
    return pl.pallas_call(
        _out_proj_kernel,
        grid=(t // tm,),
        in_specs=[
            pl.BlockSpec((tm, WIDTH_A), row),
            pl.BlockSpec((tm, WIDTH_B), row),
            pl.BlockSpec((tm, D_MODEL), row),
            _resident((WIDTH_A + WIDTH_B, D_MODEL)),
            _resident((1, D_MODEL)),
        ],
        out_specs=[pl.BlockSpec((tm, D_MODEL), row), pl.BlockSpec((tm, D_MODEL), row)],
        out_shape=[jax.ShapeDtypeStruct((t, D_MODEL), F32), jax.ShapeDtypeStruct((t, D_MODEL), BF16)],
        compiler_params=_params(1),
        name="out_proj",
    )(mixed_a, mixed_b, x2, w_out, ffn_norm)


def _gelu_tanh(x):
    k = -2.0 * math.sqrt(2.0 / math.pi) * math.log2(math.e)
    return x / (1.0 + jnp.exp2(x * (k + (k * 0.044715) * (x * x))))


def _ffn_kernel(n_ref, prev_ref, next_ref, h_ref, p_ref, wup_ref, cw_ref, cb_ref, wdn_ref,
                pn_ref, wg_ref, wp_ref, fn_ref, y_ref, ext_ref, u_ref, act_ref, acc_ref,
                *, tiles_per_seq):
    i = pl.program_id(0)
    tm = n_ref.shape[0]
    pos = i % tiles_per_seq
    prev = prev_ref[...]
    nxt = next_ref[...]
    ext_ref[0:HALO] = jnp.where(pos != 0, prev, jnp.zeros_like(prev))
    ext_ref[HALO:HALO + tm] = n_ref[...]
    ext_ref[HALO + tm:] = jnp.where(pos != tiles_per_seq - 1, nxt, jnp.zeros_like(nxt))

    def chunk_cols(ref, c):
        lo = c * FF_CHUNK
        return jnp.concatenate([ref[:, lo:lo + FF_CHUNK], ref[:, D_FF + lo:D_FF + lo + FF_CHUNK]], axis=1)

    for t in range(N_FF_CHUNKS + 2):
        if t < N_FF_CHUNKS:
            u_ref[t % 2] = jnp.dot(ext_ref[...], chunk_cols(wup_ref, t), preferred_element_type=F32)
        if 1 <= t <= N_FF_CHUNKS:
            c = t - 1
            cw = chunk_cols(cw_ref, c)
            conv = (u_ref[c % 2, HALO - 1:HALO - 1 + tm] * cw[0:1]
                    + u_ref[c % 2, HALO:HALO + tm] * cw[1:2]
                    + u_ref[c % 2, HALO + 1:HALO + 1 + tm] * cw[2:3]
                    + chunk_cols(cb_ref, c))
            act = conv[:, :FF_CHUNK] * _gelu_tanh(conv[:, FF_CHUNK:])
            act_ref[c % ACT_SLOTS] = act.astype(BF16)
        if t >= 2:
            c = t - 2
            first = c - c % DOWN_GROUP
            if c == first + DOWN_GROUP - 1 or c == N_FF_CHUNKS - 1:
                d = None
                for cc in range(first, c + 1):
                    part = jnp.dot(act_ref[cc % ACT_SLOTS], wdn_ref[cc * FF_CHUNK:(cc + 1) * FF_CHUNK, :],
                                   preferred_element_type=F32)
                    d = part if d is None else d + part
                if first == 0:
                    acc_ref[...] = d
                else:
                    acc_ref[...] += d


    h = h_ref[...] + acc_ref[...]
    gate = jax.nn.sigmoid(jnp.dot(_rms(h, pn_ref[...]).astype(BF16), wg_ref[...],
                                  preferred_element_type=F32))
    emb = jnp.dot(p_ref[...].astype(BF16), wp_ref[...], preferred_element_type=F32)
    h = h + gate * emb
    y_ref[...] = _rms(h, fn_ref[...])


def _ffn(n2, h1, p2, w_up, conv_w, conv_b, w_down, ple_norm, w_gate, w_ple, final_norm, seq):
    t = n2.shape[0]
    tm = TM_FFN
    assert t % tm == 0 and seq % tm == 0 and tm % HALO == 0
    tiles_per_seq = seq // tm
    halo_per_tile = tm // HALO
    n_halo_blocks = t // HALO
    row = lambda i: (i, 0)
    return pl.pallas_call(
        functools.partial(_ffn_kernel, tiles_per_seq=tiles_per_seq),
        grid=(t // tm,),
        in_specs=[
            pl.BlockSpec((tm, D_MODEL), row),
            pl.BlockSpec((HALO, D_MODEL), lambda i: (jnp.maximum(i * halo_per_tile - 1, 0), 0)),
            pl.BlockSpec((HALO, D_MODEL),
                         lambda i: (jnp.minimum((i + 1) * halo_per_tile, n_halo_blocks - 1), 0)),
            pl.BlockSpec((tm, D_MODEL), row),
            pl.BlockSpec((tm, PLE_DIM), row),
            _resident((D_MODEL, 2 * D_FF)),
            _resident((3, 2 * D_FF)),
            _resident((1, 2 * D_FF)),
            _resident((D_FF, D_MODEL)),
            _resident((1, D_MODEL)),
            _resident((D_MODEL, D_MODEL)),
            _resident((PLE_DIM, D_MODEL)),
            _resident((1, D_MODEL)),
        ],
        out_specs=pl.BlockSpec((tm, D_MODEL), row),
        out_shape=jax.ShapeDtypeStruct((t, D_MODEL), F32),
        scratch_shapes=[pltpu.VMEM((tm + 2 * HALO, D_MODEL), BF16),
                        pltpu.VMEM((2, tm + 2 * HALO, 2 * FF_CHUNK), F32),
                        pltpu.VMEM((ACT_SLOTS, tm, FF_CHUNK), BF16),
                        pltpu.VMEM((tm, D_MODEL), F32)],
        compiler_params=_params(1),
        name="ffn",
    )(n2, n2, n2, h1, p2, w_up, conv_w, conv_b, w_down, ple_norm, w_gate, w_ple, final_norm)


def _pair_heads(w, axis):
    group = N_Q_A // N_KV_A
    shape = w.shape
    split = shape[:axis] + (N_KV_A, group, HEAD_DIM) + shape[axis + 1:]
    return jnp.swapaxes(w.reshape(split), axis, axis + 1).reshape(shape)


def _rope_tables(seq):
    rows = seq // GRID_W
    freqs = ROPE_THETA ** (-jnp.arange(0, ROPE_AXIS_DIM, 2, dtype=F32) / ROPE_AXIS_DIM)
    ang = jnp.arange(max(rows, GRID_W), dtype=F32)[:, None] * freqs[None]
    cos_i, sin_i = jnp.cos(ang), jnp.sin(ang)

    def expand(first, second):
        pair = jnp.concatenate([first, second], axis=1)
        by_row = jnp.broadcast_to(pair[:rows, None, :], (rows, GRID_W, ROPE_AXIS_DIM))
        by_col = jnp.broadcast_to(pair[None, :GRID_W, :], (rows, GRID_W, ROPE_AXIS_DIM))
        return jnp.concatenate([by_row, by_col], axis=2).reshape(seq, HEAD_DIM)

    cos_h = expand(cos_i, cos_i)
    sin_h = expand(-sin_i, sin_i)
    return jnp.concatenate([cos_h, cos_h], axis=1), jnp.concatenate([sin_h, sin_h], axis=1)


def _prepare_params(attn_norm, w_in, q_norm_a, k_norm_a, rpb_b, out_norm_a, out_norm_b, w_out,
                    ffn_norm, w_up, conv_w, conv_b, w_down, ple_norm, w_ple_gate, w_ple, final_norm):
    head_id = np.arange(WIDTH_A) // HEAD_DIM
    head_mean = jnp.asarray((head_id[:, None] == head_id[None, :]) / HEAD_DIM, BF16)
    row2 = lambda v: v.reshape(1, -1).astype(F32)
    w_in_b = w_in[0].astype(BF16)
    w_out_b = w_out[0].astype(BF16)
    return dict(
        attn_norm=row2(attn_norm[0]),
        w_in=jnp.concatenate([_pair_heads(w_in_b[:, :WIDTH_A], 1), w_in_b[:, WIDTH_A:]], axis=1),
        q_gain=row2(jnp.tile(q_norm_a[0], N_Q_A)),
        k_gain=row2(jnp.tile(k_norm_a[0], N_KV_A)),
        head_mean=head_mean,
        na_bias=_na_bias_table(rpb_b[0]),
        gain_a=row2(_pair_heads(out_norm_a[0], 0)),
        gain_b=row2(out_norm_b[0]),
        w_out=jnp.concatenate([_pair_heads(w_out_b[:WIDTH_A], 0), w_out_b[WIDTH_A:]], axis=0),
        ffn_norm=row2(ffn_norm[0]),
        w_up=w_up[0].astype(BF16),
        conv_w=conv_w[0].astype(F32),
        conv_b=row2(conv_b[0]),
        w_down=w_down[0].astype(BF16),
        ple_norm=row2(ple_norm[0]),
        w_gate=w_ple_gate[0].astype(BF16),
        w_ple=w_ple[0].astype(BF16),
        final_norm=row2(final_norm),
    )


def _trunk(x, p, w, rope):
    batch, seq, _ = x.shape
    t = batch * seq
    x2 = x.reshape(t, D_MODEL)
    p2 = p[0].reshape(t, PLE_DIM)
    cos_t, sin_t = rope
    assert cos_t.shape[0] >= seq
    qa, ka, va, qb, kb, vb = _in_proj(x2, w["attn_norm"], w["w_in"], w["q_gain"], w["k_gain"],
                                      cos_t, sin_t, w["head_mean"], seq)
    mixed_a = _gqa(qa, ka, va, w["gain_a"], batch, seq)
    mixed_b = _natten(qb, kb, vb, w["na_bias"], w["gain_b"], batch, seq)
    h1, n2 = _out_proj(mixed_a, mixed_b, x2, w["w_out"], w["ffn_norm"])
    y = _ffn(n2, h1, p2, w["w_up"], w["conv_w"], w["conv_b"], w["w_down"], w["ple_norm"],
             w["w_gate"], w["w_ple"], w["final_norm"], seq)
    return y.reshape(batch, seq, D_MODEL)


def kernel(x_prompt, x_sample, p_prompt, p_sample, attn_norm, w_in, q_norm_a, k_norm_a, rpb_b,
           out_norm_a, out_norm_b, w_out, ffn_norm, w_up, conv_w, conv_b, w_down, ple_norm,
           w_ple_gate, w_ple, final_norm):
    w = _prepare_params(attn_norm, w_in, q_norm_a, k_norm_a, rpb_b, out_norm_a, out_norm_b, w_out,
                        ffn_norm, w_up, conv_w, conv_b, w_down, ple_norm, w_ple_gate, w_ple, final_norm)
    rope = _rope_tables(max(x_prompt.shape[1], x_sample.shape[1]))
    return (_trunk(x_prompt, p_prompt, w, rope), _trunk(x_sample, p_sample, w, rope))
```
